```python
import math
import jax, jax.numpy as jnp
from jax import lax
import numpy as np

D_MODEL = 2048
BATCH = 4
SEQ = 2048
DEPTH = 4
DEC_BATCH = 128
DEC_SEQ = 1
PAST_LEN = 16384
PAGE_SIZE = 128

CHUNK = 128
GM_GROUPS = 4
GM_WIDTH = D_MODEL // 2
GM_GW = GM_WIDTH // GM_GROUPS
CV_WIDTH = D_MODEL // 2
CONV_K = 31
N_BRANCH = 2
IN_WIDTH = 2 * GM_WIDTH + 2 * CV_WIDTH + N_BRANCH * D_MODEL
MEM_LEN = 256
X_HEADS = 4
X_HEAD_DIM = D_MODEL // X_HEADS
N_EXPERTS = 16
N_GROUPS = 4
EXPERTS_PER_GROUP = N_EXPERTS // N_GROUPS
TOP_K = 2
D_EXPERT = D_MODEL // 2
ALPHA = (2 * DEPTH) ** 0.25
BETA = (8 * DEPTH) ** -0.25
LN_EPS = 1e-5

kernel_name = "gated_gmlp_conformer_mem_moe_step"


def layer_norm(x, g, b):
    xf = x.astype(jnp.float32)
    mu = xf.mean(-1, keepdims=True)
    var = jnp.square(xf - mu).mean(-1, keepdims=True)
    return ((xf - mu) * lax.rsqrt(var + LN_EPS)).astype(x.dtype) * g + b


def chunk_spatial_gate(u, v, w_s, b_s):
    n_b, t_len, _ = v.shape
    blk = min(t_len, CHUNK)
    n_blk = t_len // blk
    vc = v.reshape(n_b, n_blk, blk, GM_GROUPS, GM_GW)
    w = jnp.tril(w_s[:, :blk, :blk])
    mixed = jnp.einsum('gts,bnsgc->bntgc', w, vc) + b_s[:, :blk].T[:, :, None]
    return u * mixed.reshape(n_b, t_len, GM_WIDTH)


def causal_depthwise_conv(ext, w_dw, b_dw):
    ch = ext.shape[-1]
    out = lax.conv_general_dilated(ext, w_dw[:, None, :].astype(ext.dtype), window_strides=(1,),
                                   padding='VALID', dimension_numbers=('NWC', 'WIO', 'NWC'),
                                   feature_group_count=ch)
    return out + b_dw


def mixer_sublayer(x, conv_past, w_in, b_in, gm_ln_g, gm_ln_b, gm_w_s, gm_b_s, w_a,
                   conv_w, conv_b, cv_ln_g, cv_ln_b, w_b, w_o):
    z = x @ w_in + b_in
    uv = jax.nn.gelu(z[..., :2 * GM_WIDTH])
    u, v = uv[..., :GM_WIDTH], uv[..., GM_WIDTH:]
    v = layer_norm(v, gm_ln_g, gm_ln_b)
    branch_a = chunk_spatial_gate(u, v, gm_w_s, gm_b_s) @ w_a
    zc = z[..., 2 * GM_WIDTH:2 * GM_WIDTH + 2 * CV_WIDTH]
    c = zc[..., :CV_WIDTH] * jax.nn.sigmoid(zc[..., CV_WIDTH:])
    ext = jnp.concatenate([conv_past.astype(c.dtype), c], axis=1)
    h = causal_depthwise_conv(ext, conv_w, conv_b)
    h = jax.nn.silu(layer_norm(h, cv_ln_g, cv_ln_b))
    branch_b = h @ w_b
    gates = jax.nn.sigmoid(z[..., 2 * GM_WIDTH + 2 * CV_WIDTH:])
    merged = gates[..., :D_MODEL] * branch_a + gates[..., D_MODEL:] * branch_b
    return merged @ w_o, v, ext[:, -(CONV_K - 1):]


def mem_attend(x, mem_k, mem_v, w_q, w_xo):
    n_b, t_len, _ = x.shape
    q = (x @ w_q).reshape(n_b, t_len, X_HEADS, X_HEAD_DIM)
    s = jnp.einsum('bthd,bmhd->bhtm', q, mem_k).astype(jnp.float32) * (X_HEAD_DIM ** -0.5)
    p = jax.nn.softmax(s, axis=-1).astype(x.dtype)
    o = jnp.einsum('bhtm,bmhd->bthd', p, mem_v).reshape(n_b, t_len, D_MODEL)
    return o @ w_xo


def routed_ffn(x, w_router, b_router, w_gate, w_up, w_down):
    n_b, t_len, _ = x.shape
    xt = x.reshape(-1, D_MODEL)
    aff = jax.nn.sigmoid((xt @ w_router).astype(jnp.float32))
    sel = (aff + b_router.astype(jnp.float32)).reshape(-1, N_GROUPS, EXPERTS_PER_GROUP)
    grp_score = lax.top_k(sel, TOP_K)[0].sum(-1)
    g_best = jnp.argmax(grp_score, axis=-1)
    in_group = jnp.take_along_axis(sel, g_best[:, None, None], axis=1)[:, 0]
    _, loc = lax.top_k(in_group, TOP_K)
    idx = g_best[:, None] * EXPERTS_PER_GROUP + loc
    wts = jnp.take_along_axis(aff, idx, axis=-1)
    wts = wts / wts.sum(-1, keepdims=True)
    combine = (jax.nn.one_hot(idx, N_EXPERTS, dtype=jnp.float32) * wts[..., None]).sum(1).astype(x.dtype)
    h = jax.nn.silu(jnp.einsum('md,edf->mef', xt, w_gate)) * jnp.einsum('md,edf->mef', xt, w_up)
    y = jnp.einsum('mef,efd->md', h * combine[..., None], w_down)
    return y.reshape(n_b, t_len, D_MODEL)


def setup_inputs(seed: int = 0) -> dict:
    key = jax.random.key(seed)
    ks = iter(jax.random.split(key, 48))

    def nrm(shape, scale):
        return jax.random.normal(next(ks), shape, jnp.float32) * scale

    def gain(shape):
        return 1.0 + nrm(shape, 0.02)

    L = DEPTH
    return {
        "x_prompt": nrm((BATCH, SEQ, D_MODEL), 1.0),
        "x_sample": nrm((DEC_BATCH, DEC_SEQ, D_MODEL), 1.0),
        "mem_prompt": nrm((BATCH, MEM_LEN, D_MODEL), 1.0),
        "cache_conv": nrm((L, DEC_BATCH, CONV_K - 1, CV_WIDTH), 0.5),
        "cache_mem_k": nrm((L, DEC_BATCH, MEM_LEN, X_HEADS, X_HEAD_DIM), 1.0),
        "cache_mem_v": nrm((L, DEC_BATCH, MEM_LEN, X_HEADS, X_HEAD_DIM), 1.0),
        "ln_in_g": gain((D_MODEL,)),
        "ln_in_b": nrm((D_MODEL,), 0.02),
        "w_in": nrm((L, D_MODEL, IN_WIDTH), D_MODEL ** -0.5),
        "b_in": nrm((L, IN_WIDTH), 0.02),
        "gm_ln_g": gain((L, GM_WIDTH)),
        "gm_ln_b": nrm((L, GM_WIDTH), 0.02),
        "gm_w_s": nrm((L, GM_GROUPS, CHUNK, CHUNK), CHUNK ** -0.5),
        "gm_b_s": gain((L, GM_GROUPS, CHUNK)),
        "w_a": nrm((L, GM_WIDTH, D_MODEL), GM_WIDTH ** -0.5),
        "conv_w": nrm((L, CONV_K, CV_WIDTH), CONV_K ** -0.5),
        "conv_b": nrm((L, CV_WIDTH), 0.02),
        "cv_ln_g": gain((L, CV_WIDTH)),
        "cv_ln_b": nrm((L, CV_WIDTH), 0.02),
        "w_b": nrm((L, CV_WIDTH, D_MODEL), CV_WIDTH ** -0.5),
        "w_o": nrm((L, D_MODEL, D_MODEL), BETA * D_MODEL ** -0.5),
        "ln1_g": gain((L, D_MODEL)),
        "ln1_b": nrm((L, D_MODEL), 0.02),
        "w_q": nrm((L, D_MODEL, D_MODEL), D_MODEL ** -0.5),
        "w_k": nrm((L, D_MODEL, D_MODEL), D_MODEL ** -0.5),
        "w_v": nrm((L, D_MODEL, D_MODEL), D_MODEL ** -0.5),
        "w_xo": nrm((L, D_MODEL, D_MODEL), BETA * D_MODEL ** -0.5),
        "ln2_g": gain((L, D_MODEL)),
        "ln2_b": nrm((L, D_MODEL), 0.02),
        "w_router": nrm((D_MODEL, N_EXPERTS), D_MODEL ** -0.5),
        "b_router": nrm((N_EXPERTS,), 0.01),
        "w_gate": nrm((L, N_EXPERTS, D_MODEL, D_EXPERT), D_MODEL ** -0.5),
        "w_up": nrm((L, N_EXPERTS, D_MODEL, D_EXPERT), D_MODEL ** -0.5),
        "w_down": nrm((L, N_EXPERTS, D_EXPERT, D_MODEL), BETA * D_EXPERT ** -0.5),
        "ln3_g": gain((L, D_MODEL)),
        "ln3_b": nrm((L, D_MODEL), 0.02),
    }


def reference(x_prompt, x_sample, mem_prompt, cache_conv, cache_mem_k, cache_mem_v,
              ln_in_g, ln_in_b, w_in, b_in, gm_ln_g, gm_ln_b, gm_w_s, gm_b_s, w_a,
              conv_w, conv_b, cv_ln_g, cv_ln_b, w_b, w_o, ln1_g, ln1_b,
              w_q, w_k, w_v, w_xo, ln2_g, ln2_b, w_router, b_router,
              w_gate, w_up, w_down, ln3_g, ln3_b):

    def run_layer(l, x, conv_past, mem_k, mem_v):
        m, v_rows, new_conv = mixer_sublayer(x, conv_past, w_in[l], b_in[l], gm_ln_g[l], gm_ln_b[l],
                                             gm_w_s[l], gm_b_s[l], w_a[l], conv_w[l], conv_b[l],
                                             cv_ln_g[l], cv_ln_b[l], w_b[l], w_o[l])
        x = layer_norm(ALPHA * x + m, ln1_g[l], ln1_b[l])
        x = layer_norm(ALPHA * x + mem_attend(x, mem_k, mem_v, w_q[l], w_xo[l]), ln2_g[l], ln2_b[l])
        x = layer_norm(ALPHA * x + routed_ffn(x, w_router, b_router, w_gate[l], w_up[l], w_down[l]),
                       ln3_g[l], ln3_b[l])
        return x, v_rows, new_conv

    n_p = x_prompt.shape[0]
    xp = layer_norm(x_prompt, ln_in_g, ln_in_b)
    xs = layer_norm(x_sample, ln_in_g, ln_in_b)
    conv_zero = jnp.zeros((n_p, CONV_K - 1, CV_WIDTH), x_prompt.dtype)
    mk_list, mv_list, cp_list, cs_list, vs_list = [], [], [], [], []
    for l in range(DEPTH):
        mk = (mem_prompt @ w_k[l]).reshape(n_p, MEM_LEN, X_HEADS, X_HEAD_DIM)
        mv = (mem_prompt @ w_v[l]).reshape(n_p, MEM_LEN, X_HEADS, X_HEAD_DIM)
        xp, _, conv_p = run_layer(l, xp, conv_zero, mk, mv)
        xs, v_s, conv_s = run_layer(l, xs, cache_conv[l], cache_mem_k[l], cache_mem_v[l])
        mk_list.append(mk)
        mv_list.append(mv)
        cp_list.append(conv_p)
        cs_list.append(conv_s)
        vs_list.append(v_s)
    mem_k_prompt = jnp.stack(mk_list)
    mem_v_prompt = jnp.stack(mv_list)
    conv_state_prompt = jnp.stack(cp_list)
    conv_state_sample = jnp.stack(cs_list)
    gm_v_sample = jnp.stack(vs_list)
    return (xp, xs, mem_k_prompt, mem_v_prompt, conv_state_prompt, conv_state_sample, gm_v_sample)
```

```python
import functools

import jax
import jax.numpy as jnp
from jax import lax
from jax.experimental import pallas as pl
from jax.experimental.pallas import tpu as pltpu

F32 = jnp.float32
BF16 = jnp.bfloat16
U32 = jnp.uint32
I32 = jnp.int32

D_MODEL = 2048
BATCH = 4
SEQ = 2048
DEPTH = 4
DEC_BATCH = 128
CHUNK = 128
GM_GROUPS = 4
GM_WIDTH = D_MODEL // 2
GM_GW = GM_WIDTH // GM_GROUPS
CV_WIDTH = D_MODEL // 2
CONV_K = 31
MEM_LEN = 256
X_HEADS = 4
X_HEAD_DIM = D_MODEL // X_HEADS
N_EXPERTS = 16
N_GROUPS = 4
EPG = N_EXPERTS // N_GROUPS
D_EXPERT = D_MODEL // 2
ALPHA = (2 * DEPTH) ** 0.25
LN_EPS = 1e-5

M_PROMPT = BATCH * SEQ
M_ALL = M_PROMPT + DEC_BATCH
TM = 640
N_MT = M_ALL // TM
CH = 320
LANES = 128
HALO = 32
TME = 256
N_ET = (2 * M_ALL + N_EXPERTS * (TME - 1) + TME - 1) // TME
M_SORT = N_ET * TME
XP_S = (D_MODEL // 2) // LANES
Y_S = D_MODEL // LANES
TF = 512
VMEM_LIMIT = 56 * 1024 * 1024


def _cparams(n_grid):
    return pltpu.CompilerParams(dimension_semantics=("arbitrary",) * n_grid,
                                vmem_limit_bytes=VMEM_LIMIT)


def _const_spec(shape, single=False):
    nd = len(shape)
    if single:
        return pl.BlockSpec(shape, lambda *_: (0,) * nd, pipeline_mode=pl.Buffered(1))
    return pl.BlockSpec(shape, lambda *_: (0,) * nd)


def _ln(x, g, b):
    mu = jnp.mean(x, axis=-1, keepdims=True)
    xc = x - mu
    var = jnp.mean(xc * xc, axis=-1, keepdims=True)
    return xc * lax.rsqrt(var + LN_EPS) * g + b


def _ln_in_body(x_ref, g_ref, b_ref, of_ref, ob_ref):
    for c in range(TM // CHUNK):
        rows = pl.ds(c * CHUNK, CHUNK)
        y = _ln(x_ref[rows, :], g_ref[...], b_ref[...])
        of_ref[rows, :] = y
        ob_ref[rows, :] = y.astype(BF16)


def _ln_in(x, g, b):
    row = pl.BlockSpec((TM, D_MODEL), lambda i: (i, 0))
    return pl.pallas_call(
        _ln_in_body, grid=(N_MT,),
        in_specs=[row, _const_spec((1, D_MODEL)), _const_spec((1, D_MODEL))],
        out_specs=[row, row],
        out_shape=[jax.ShapeDtypeStruct((M_ALL, D_MODEL), F32),
                   jax.ShapeDtypeStruct((M_ALL, D_MODEL), BF16)],
        compiler_params=_cparams(1), name="ln_in")(x, g, b)


def _proj_body(n_par, epilogue, x_ref, w_ref, b_ref, *rest):
    pars = rest[:n_par]
    outs = rest[n_par:]
    acc = jnp.dot(x_ref[...], w_ref[...], preferred_element_type=F32) + b_ref[...]
    res = epilogue(acc, *[p[...] for p in pars])
    for o, r in zip(outs, res):
        o[...] = r.astype(o.dtype)


def _proj(x, w, b, epilogue, out_dtypes, *, tn, params=(), tm=TM, name):
    m, k = x.shape
    n = w.shape[1]
    grid = (n // tn, m // tm)
    colp = pl.BlockSpec((1, tn), lambda j, i: (0, j))
    out_spec = pl.BlockSpec((tm, tn), lambda j, i: (i, j))
    return pl.pallas_call(
        functools.partial(_proj_body, len(params), epilogue), grid=grid,
        in_specs=[pl.BlockSpec((tm, k), lambda j, i: (i, 0)),
                  pl.BlockSpec((k, tn), lambda j, i: (0, j)),
                  colp] + [colp] * len(params),
        out_specs=[out_spec] * len(out_dtypes),
        out_shape=[jax.ShapeDtypeStruct((m, n), d) for d in out_dtypes],
        compiler_params=_cparams(2), name=name)(x, w, b, *params)


def _glu_body(x_ref, wa_ref, wb_ref, ba_ref, bb_ref, o_ref):
    x = x_ref[...]
    a = jnp.dot(x, wa_ref[...], preferred_element_type=F32) + ba_ref[...]
    g = jnp.dot(x, wb_ref[...], preferred_element_type=F32) + bb_ref[...]
    o_ref[...] = a * jax.nn.sigmoid(g)


def _glu(x, wa, wb, ba, bb):
    tn = 512
    grid = (CV_WIDTH // tn, N_MT)
    wspec = pl.BlockSpec((D_MODEL, tn), lambda j, i: (0, j))
    bspec = pl.BlockSpec((1, tn), lambda j, i: (0, j))
    return pl.pallas_call(
        _glu_body, grid=grid,
        in_specs=[pl.BlockSpec((TM, D_MODEL), lambda j, i: (i, 0)), wspec, wspec, bspec, bspec],
        out_specs=pl.BlockSpec((TM, tn), lambda j, i: (i, j)),
        out_shape=jax.ShapeDtypeStruct((M_ALL, CV_WIDTH), F32),
        compiler_params=_cparams(2), name="glu")(x, wa, wb, ba, bb)


def _gate_body(u_ref, v_ref, ws_ref, bs_ref, o_ref):
    i = pl.program_id(0)
    r = lax.broadcasted_iota(I32, (CHUNK, CHUNK), 0)
    c = lax.broadcasted_iota(I32, (CHUNK, CHUNK), 1)
    n_chunk = TM // CHUNK
    for j in range(n_chunk):
        rows = pl.ds(j * CHUNK, CHUNK)
        for g in range(GM_GROUPS):
            cols = pl.ds(g * GM_GW, GM_GW)
            w = ws_ref[g]
            bias = bs_ref[:, g:g + 1]
            wm = jnp.where(r >= c, w, 0.0)
            if j == n_chunk - 1:
                is_s = i == N_MT - 1
                wm = jnp.where(is_s, jnp.where(r == c, w[0:1, 0:1], 0.0), wm)
                bias = jnp.where(is_s, bias[0:1, :], bias)
            mixed = jnp.dot(wm.astype(BF16), v_ref[rows, cols].astype(BF16),
                            preferred_element_type=F32) + bias
            o_ref[rows, cols] = (u_ref[rows, cols].astype(F32) * mixed).astype(BF16)


def _gate(u, v, w_s, b_s_t):
    row = pl.BlockSpec((TM, GM_WIDTH), lambda i: (i, 0))
    return pl.pallas_call(
        _gate_body, grid=(N_MT,),
        in_specs=[row, row, _const_spec((GM_GROUPS, CHUNK, CHUNK)), _const_spec((CHUNK, GM_GROUPS))],
        out_specs=row,
        out_shape=jax.ShapeDtypeStruct((M_ALL, GM_WIDTH), BF16),
        compiler_params=_cparams(1), name="spatial_gate")(u, v, w_s, b_s_t)


def _conv_body(c_ref, halo_ref, w_ref, b_ref, g_ref, be_ref, o_ref, ext_ref, acc_ref):
    i = pl.program_id(0)
    ext_ref[0:HALO, :] = halo_ref[...]
    ext_ref[HALO:, :] = c_ref[...]
    n_chunk = TM // CHUNK
    n_cb = CV_WIDTH // LANES
    row_id = lax.broadcasted_iota(I32, (CHUNK + HALO, LANES), 0)

    for j in range(n_chunk):
        opens = ((i * n_chunk + j) % (SEQ // CHUNK)) == 0

        def chan_block(cb, carry):
            cols = pl.ds(pl.multiple_of(cb * LANES, LANES), LANES)
            ext = ext_ref[pl.ds(j * CHUNK, CHUNK + HALO), cols]
            ext = jnp.where(jnp.logical_and(opens, row_id < HALO), 0.0, ext)
            acc = jnp.zeros((CHUNK, LANES), F32) + b_ref[:, cols]
            for k in range(CONV_K):
                s = HALO - (CONV_K - 1) + k
                acc = acc + w_ref[k:k + 1, cols] * ext[s:s + CHUNK, :]
            acc_ref[:, cols] = acc
            return carry

        lax.fori_loop(0, n_cb, chan_block, 0)
        h = _ln(acc_ref[...], g_ref[...], be_ref[...])
        o_ref[pl.ds(j * CHUNK, CHUNK), :] = (h * jax.nn.sigmoid(h)).astype(BF16)


def _conv(c, w, b, g, be):
    blocks_per_tile = TM // HALO
    return pl.pallas_call(
        _conv_body, grid=(N_MT,),
        in_specs=[pl.BlockSpec((TM, CV_WIDTH), lambda i: (i, 0)),
                  pl.BlockSpec((HALO, CV_WIDTH), lambda i: (jnp.maximum(i * blocks_per_tile - 1, 0), 0)),
                  _const_spec((CONV_K, CV_WIDTH)), _const_spec((1, CV_WIDTH)),
                  _const_spec((1, CV_WIDTH)), _const_spec((1, CV_WIDTH))],
        out_specs=pl.BlockSpec((TM, CV_WIDTH), lambda i: (i, 0)),
        out_shape=jax.ShapeDtypeStruct((M_ALL, CV_WIDTH), BF16),
        scratch_shapes=[pltpu.VMEM((TM + HALO, CV_WIDTH), F32), pltpu.VMEM((CHUNK, CV_WIDTH), F32)],
        compiler_params=_cparams(1), name="conv_prompt")(c, c, w, b, g, be)


CONV_TB = 16


def _conv_s_body(cache_ref, c_ref, w_ref, b_ref, g_ref, be_ref, o_ref):
    hist = jnp.sum(cache_ref[...] * w_ref[0:CONV_K - 1, :][None], axis=1)
    acc = hist + c_ref[...] * w_ref[CONV_K - 1:CONV_K, :] + b_ref[...]
    h = _ln(acc, g_ref[...], be_ref[...])
    o_ref[...] = (h * jax.nn.sigmoid(h)).astype(BF16)


def _conv_sample(cache, l, c, w, b, g, be):
    first = M_PROMPT // CONV_TB
    return pl.pallas_call(
        _conv_s_body, grid=(DEC_BATCH // CONV_TB,),
        in_specs=[pl.BlockSpec((None, CONV_TB, CONV_K - 1, CV_WIDTH), lambda i: (l, i, 0, 0)),
                  pl.BlockSpec((CONV_TB, CV_WIDTH), lambda i: (first + i, 0)),
                  _const_spec((CONV_K, CV_WIDTH)), _const_spec((1, CV_WIDTH)),
                  _const_spec((1, CV_WIDTH)), _const_spec((1, CV_WIDTH))],
        out_specs=pl.BlockSpec((CONV_TB, CV_WIDTH), lambda i: (i, 0)),
        out_shape=jax.ShapeDtypeStruct((DEC_BATCH, CV_WIDTH), BF16),
        compiler_params=_cparams(1), name="conv_sample")(cache, c, w, b, g, be)


def _merge_body(sg_ref, h_ref, wa_ref, wb_ref, ga_ref, gb_ref, o_ref):
    a = jnp.dot(sg_ref[...], wa_ref[...], preferred_element_type=F32)
    b = jnp.dot(h_ref[...], wb_ref[...], preferred_element_type=F32)
    o_ref[...] = (ga_ref[...].astype(F32) * a + gb_ref[...].astype(F32) * b).astype(BF16)


def _merge(sg, h, wa, wb, gates):
    tn = 1024
    nb = D_MODEL // tn
    lhs = pl.BlockSpec((TM, GM_WIDTH), lambda j, i: (i, 0))
    wspec = pl.BlockSpec((GM_WIDTH, tn), lambda j, i: (0, j))
    return pl.pallas_call(
        _merge_body, grid=(nb, N_MT),
        in_specs=[lhs, lhs, wspec, wspec,
                  pl.BlockSpec((TM, tn), lambda j, i: (i, j)),
                  pl.BlockSpec((TM, tn), lambda j, i: (i, j + nb))],
        out_specs=pl.BlockSpec((TM, tn), lambda j, i: (i, j)),
        out_shape=jax.ShapeDtypeStruct((M_ALL, D_MODEL), BF16),
        compiler_params=_cparams(2), name="merge")(sg, h, wa, wb, gates, gates)


def _proj_ln_body(a_ref, w_ref, r_ref, g_ref, b_ref, of_ref, ob_ref):
    for c in range(TM // CH):
        rows = pl.ds(c * CH, CH)
        y = jnp.dot(a_ref[rows, :], w_ref[...], preferred_element_type=F32)
        z = _ln(ALPHA * r_ref[rows, :] + y, g_ref[...], b_ref[...])
        of_ref[rows, :] = z
        ob_ref[rows, :] = z.astype(BF16)


def _proj_ln(a, w, r, g, b, name):
    row = pl.BlockSpec((TM, D_MODEL), lambda i: (i, 0))
    return pl.pallas_call(
        _proj_ln_body, grid=(N_MT,),
        in_specs=[row, _const_spec((D_MODEL, D_MODEL), single=True), row,
                  _const_spec((1, D_MODEL)), _const_spec((1, D_MODEL))],
        out_specs=[row, row],
        out_shape=[jax.ShapeDtypeStruct((M_ALL, D_MODEL), F32),
                   jax.ShapeDtypeStruct((M_ALL, D_MODEL), BF16)],
        compiler_params=_cparams(1), name=name)(a, w, r, g, b)


def _pack_pair(lo, hi):
    lo_b = pltpu.bitcast(lo.astype(BF16).astype(F32), U32)
    hi_b = pltpu.bitcast(hi.astype(BF16).astype(F32), U32)
    return lax.shift_right_logical(lo_b, jnp.uint32(16)) | (hi_b & jnp.uint32(0xFFFF0000))


def _unpack_pair(p):
    lo = pltpu.bitcast(lax.shift_left(p, jnp.uint32(16)), F32)
    hi = pltpu.bitcast(p & jnp.uint32(0xFFFF0000), F32)
    return lo.astype(BF16), hi.astype(BF16)


def _proj_ln_route_body(a_ref, w_ref, r_ref, g_ref, b_ref, wr_ref, of_ref, xp_ref, lg_ref):
    half = D_MODEL // 2
    for c in range(TM // CH):
        rows = pl.ds(c * CH, CH)
        y = jnp.dot(a_ref[rows, :], w_ref[...], preferred_element_type=F32)
        z = _ln(ALPHA * r_ref[rows, :] + y, g_ref[...], b_ref[...])
        of_ref[rows, :] = z
        packed = _pack_pair(z[:, :half], z[:, half:])
        for s in range(XP_S):
            xp_ref[pl.ds(c * CH * XP_S + s, CH, stride=XP_S), :] = packed[:, s * LANES:(s + 1) * LANES]
    lg_ref[...] = lax.dot_general(wr_ref[...], of_ref[...], (((1,), (1,)), ((), ())),
                                  precision=lax.Precision.HIGHEST, preferred_element_type=F32)


def _proj_ln_route(a, w, r, g, b, wr_t):
    row = pl.BlockSpec((TM, D_MODEL), lambda i: (i, 0))
    return pl.pallas_call(
        _proj_ln_route_body, grid=(N_MT,),
        in_specs=[row, _const_spec((D_MODEL, D_MODEL), single=True), row,
                  _const_spec((1, D_MODEL)), _const_spec((1, D_MODEL)),
                  _const_spec((N_EXPERTS, D_MODEL))],
        out_specs=[row, pl.BlockSpec((TM * XP_S, LANES), lambda i: (i, 0)),
                   pl.BlockSpec((N_EXPERTS, TM), lambda i: (0, i))],
        out_shape=[jax.ShapeDtypeStruct((M_ALL, D_MODEL), F32),
                   jax.ShapeDtypeStruct((M_ALL * XP_S, LANES), U32),
                   jax.ShapeDtypeStruct((N_EXPERTS, M_ALL), F32)],
        compiler_params=_cparams(1), name="xo_ln_route")(a, w, r, g, b, wr_t)


def _kv_body(x_ref, w_ref, of_ref, ob_ref):
    y = jnp.dot(x_ref[...], w_ref[...], preferred_element_type=F32)
    of_ref[...] = y
    ob_ref[...] = y.astype(BF16)


def _kv(mem_b, w):
    tn = 1024
    m = mem_b.shape[0]
    out = pl.BlockSpec((m, tn), lambda j: (0, j))
    return pl.pallas_call(
        _kv_body, grid=(D_MODEL // tn,),
        in_specs=[_const_spec((m, D_MODEL)), pl.BlockSpec((D_MODEL, tn), lambda j: (0, j))],
        out_specs=[out, out],
        out_shape=[jax.ShapeDtypeStruct((m, D_MODEL), F32), jax.ShapeDtypeStruct((m, D_MODEL), BF16)],
        compiler_params=_cparams(1), name="mem_kv")(mem_b, w)


TQ = 512


def _attn_body(q_ref, k_ref, v_ref, o_ref):
    scale = X_HEAD_DIM ** -0.5
    for h in range(X_HEADS):
        cols = pl.ds(h * X_HEAD_DIM, X_HEAD_DIM)
        s = lax.dot_general(q_ref[:, cols], k_ref[:, cols], (((1,), (1,)), ((), ())),
                            preferred_element_type=F32) * scale
        s = s - jnp.max(s, axis=-1, keepdims=True)
        e = jnp.exp(s)
        p = e / jnp.sum(e, axis=-1, keepdims=True)
        o = jnp.dot(p.astype(BF16), v_ref[:, cols], preferred_element_type=F32)
        o_ref[:, cols] = o.astype(BF16)


def _attn_prompt(q, k, v):
    per_b = SEQ // TQ
    kv = pl.BlockSpec((MEM_LEN, D_MODEL), lambda b, i: (b, 0))
    return pl.pallas_call(
        _attn_body, grid=(BATCH, per_b),
        in_specs=[pl.BlockSpec((TQ, D_MODEL), lambda b, i: (b * per_b + i, 0)), kv, kv],
        out_specs=pl.BlockSpec((TQ, D_MODEL), lambda b, i: (b * per_b + i, 0)),
        out_shape=jax.ShapeDtypeStruct((M_PROMPT, D_MODEL), BF16),
        compiler_params=_cparams(2), name="attn_prompt")(q, k, v)


ATT_TB = 2


def _attn_s_body(q_ref, k_ref, v_ref, o_ref):
    scale = X_HEAD_DIM ** -0.5
    for t in range(ATT_TB):
        q = q_ref[t].astype(F32)
        s = jnp.sum(k_ref[t] * q[None], axis=-1, keepdims=True) * scale
        s = s - jnp.max(s, axis=0, keepdims=True)
        e = jnp.exp(s)
        p = e / jnp.sum(e, axis=0, keepdims=True)
        o_ref[t] = jnp.sum(p * v_ref[t], axis=0)


def _attn_sample(q_s, cache_k, cache_v, l):
    kv = pl.BlockSpec((None, ATT_TB, MEM_LEN, X_HEADS, X_HEAD_DIM), lambda i: (l, i, 0, 0, 0))
    qo = pl.BlockSpec((ATT_TB, X_HEADS, X_HEAD_DIM), lambda i: (i, 0, 0))
    return pl.pallas_call(
        _attn_s_body, grid=(DEC_BATCH // ATT_TB,),
        in_specs=[qo, kv, kv], out_specs=qo,
        out_shape=jax.ShapeDtypeStruct((DEC_BATCH, X_HEADS, X_HEAD_DIM), F32),
        compiler_params=_cparams(1), name="attn_sample")(q_s, cache_k, cache_v)


def _router_body(lg_ref, br_ref, w_ref, p_ref, cnt_ref, sel_ref, rank_ref):
    aff = jax.nn.sigmoid(lg_ref[...])
    sel = aff + br_ref[...]
    srow = [sel[e:e + 1, :] for e in range(N_EXPERTS)]
    arow = [aff[e:e + 1, :] for e in range(N_EXPERTS)]

    scores = []
    for g in range(N_GROUPS):
        q = srow[g * EPG:(g + 1) * EPG]
        best = None
        for a in range(EPG):
            for b in range(a + 1, EPG):
                pair = q[a] + q[b]
                best = pair if best is None else jnp.maximum(best, pair)
        scores.append(best)
    top = functools.reduce(jnp.maximum, scores)
    taken = jnp.zeros_like(top, dtype=jnp.bool_)
    chosen = []
    for g in range(N_GROUPS):
        is_g = jnp.logical_and(scores[g] == top, jnp.logical_not(taken))
        taken = jnp.logical_or(taken, is_g)
        chosen.append(is_g)

    mask = []
    for g in range(N_GROUPS):
        q = srow[g * EPG:(g + 1) * EPG]
        for a in range(EPG):
            rank = jnp.zeros_like(top)
            for b in range(EPG):
                if b == a:
                    continue
                ahead = q[b] > q[a]
                if b < a:
                    ahead = jnp.logical_or(ahead, q[b] == q[a])
                rank = rank + ahead.astype(F32)
            mask.append(jnp.logical_and(chosen[g], rank < 2.0))

    wsel = [jnp.where(mask[e], arow[e], 0.0) for e in range(N_EXPERTS)]
    den = functools.reduce(lambda x, y: x + y, wsel)
    for e in range(N_EXPERTS):
        sel_ref[e:e + 1, :] = mask[e].astype(F32)

    tri = (lax.broadcasted_iota(I32, (LANES, LANES), 0) <=
           lax.broadcasted_iota(I32, (LANES, LANES), 1)).astype(BF16)
    run = jnp.zeros((N_EXPERTS, 1), F32)
    for blk in range(M_ALL // LANES):
        cols = pl.ds(blk * LANES, LANES)
        m = sel_ref[:, cols]
        incl = jnp.dot(m.astype(BF16), tri, preferred_element_type=F32)
        rank_ref[:, cols] = run + incl - m
        run = run + incl[:, LANES - 1:LANES]
    padded = jnp.floor((run + (TME - 1)) * (1.0 / TME)) * TME
    low = (lax.broadcasted_iota(I32, (N_EXPERTS, N_EXPERTS), 0) >
           lax.broadcasted_iota(I32, (N_EXPERTS, N_EXPERTS), 1)).astype(BF16)
    start = jnp.dot(low, jnp.broadcast_to(padded, (N_EXPERTS, LANES)).astype(BF16),
                    preferred_element_type=F32)
    cnt_ref[...] = jnp.broadcast_to(run, (N_EXPERTS, LANES))

    seen = jnp.zeros_like(taken)
    w0 = jnp.zeros_like(top)
    w1 = jnp.zeros_like(top)
    p0 = jnp.zeros_like(top)
    p1 = jnp.zeros_like(top)
    for e in range(N_EXPERTS):
        we = wsel[e] / den
        dest = rank_ref[e:e + 1, :] + start[e:e + 1, 0:1]
        first = jnp.logical_and(mask[e], jnp.logical_not(seen))
        second = jnp.logical_and(mask[e], seen)
        seen = jnp.logical_or(seen, mask[e])
        w0 = jnp.where(first, we, w0)
        w1 = jnp.where(second, we, w1)
        p0 = jnp.where(first, dest, p0)
        p1 = jnp.where(second, dest, p1)
    w_ref[0:1, :] = w0
    w_ref[1:2, :] = w1
    p_ref[0:1, :] = p0.astype(I32)
    p_ref[1:2, :] = p1.astype(I32)


def _router(logits_t, b_router):
    return pl.pallas_call(
        _router_body, grid=(1,),
        in_specs=[_const_spec((N_EXPERTS, M_ALL)), _const_spec((N_EXPERTS, 1))],
        out_specs=[_const_spec((2, M_ALL)), _const_spec((2, M_ALL)), _const_spec((N_EXPERTS, LANES))],
        out_shape=[jax.ShapeDtypeStruct((2, M_ALL), F32), jax.ShapeDtypeStruct((2, M_ALL), I32),
                   jax.ShapeDtypeStruct((N_EXPERTS, LANES), F32)],
        scratch_shapes=[pltpu.VMEM((N_EXPERTS, M_ALL), F32), pltpu.VMEM((N_EXPERTS, M_ALL), F32)],
        compiler_params=_cparams(1), name="router")(logits_t, b_router)


SC_GROUP = 640


def _scatter_body(p_ref, cnt_ref, start_ref, x_hbm, o_hbm, zero_ref, sem, zsem):
    zero_ref[...] = jnp.zeros(zero_ref.shape, U32)

    def pad_copies(fn):
        for e in range(N_EXPERTS):
            n_tok = cnt_ref[e]
            n_pad = (TME - n_tok % TME) % TME
            cur = start_ref[e] + n_tok
            bit = TME // 2
            while bit >= 1:
                take = (n_pad & bit) != 0

                @pl.when(take)
                def _(cur=cur, bit=bit):
                    fn(pltpu.make_async_copy(zero_ref.at[pl.ds(0, bit)], o_hbm.at[pl.ds(cur, bit)], zsem))

                cur = cur + jnp.where(take, bit, 0)
                bit //= 2

    last = N_EXPERTS - 1
    n_used = (start_ref[last] + cnt_ref[last] + TME - 1) // TME

    def tail_copies(fn):
        def one(ti, carry):
            fn(pltpu.make_async_copy(zero_ref, o_hbm.at[pl.ds(ti * TME, TME)], zsem))
            return carry
        lax.fori_loop(n_used, N_ET, one, 0)

    pad_copies(lambda cp: cp.start())
    tail_copies(lambda cp: cp.start())

    def group(gi, carry):
        base = gi * SC_GROUP

        def issue(r, c2):
            t = base + r
            pltpu.make_async_copy(x_hbm.at[t], o_hbm.at[p_ref[t]], sem).start()
            pltpu.make_async_copy(x_hbm.at[t], o_hbm.at[p_ref[M_ALL + t]], sem).start()
            return c2

        lax.fori_loop(0, SC_GROUP, issue, 0)
        done = pltpu.make_async_copy(x_hbm.at[pl.ds(0, SC_GROUP)], o_hbm.at[pl.ds(0, SC_GROUP)], sem)
        done.wait()
        done.wait()
        return carry

    lax.fori_loop(0, M_ALL // SC_GROUP, group, 0)
    pad_copies(lambda cp: cp.wait())
    tail_copies(lambda cp: cp.wait())


def _scatter(p_flat, cnt, start, xp3):
    return pl.pallas_call(
        _scatter_body,
        grid_spec=pltpu.PrefetchScalarGridSpec(
            num_scalar_prefetch=3, grid=(1,),
            in_specs=[pl.BlockSpec(memory_space=pl.ANY)],
            out_specs=pl.BlockSpec(memory_space=pl.ANY),
            scratch_shapes=[pltpu.VMEM((TME, XP_S, LANES), U32),
                            pltpu.SemaphoreType.DMA(()), pltpu.SemaphoreType.DMA(())]),
        out_shape=jax.ShapeDtypeStruct((M_SORT, XP_S, LANES), U32),
        compiler_params=_cparams(1), name="moe_scatter")(p_flat, cnt, start, xp3)


def _expert_up_body(eff_ref, exp_ref, valid_ref, first_ref, xs_ref, wg_ref, wu_ref, o_ref,
                    wg_s, wu_s):
    i = pl.program_id(1)

    @pl.when(first_ref[i] == 1)
    def _():
        wg_s[...] = wg_ref[...].astype(BF16)
        wu_s[...] = wu_ref[...].astype(BF16)

    @pl.when(valid_ref[i] == 1)
    def _():
        packed = jnp.concatenate([xs_ref[pl.ds(s, TME, stride=XP_S), :] for s in range(XP_S)], axis=1)
        lo, hi = _unpack_pair(packed)
        x = jnp.concatenate([lo, hi], axis=1)
        g = jnp.dot(x, wg_s[...], preferred_element_type=F32)
        u = jnp.dot(x, wu_s[...], preferred_element_type=F32)
        o_ref[...] = (g * jax.nn.sigmoid(g) * u).astype(BF16)

    @pl.when(valid_ref[i] == 0)
    def _():
        o_ref[...] = jnp.zeros(o_ref.shape, BF16)


def _expert_up(maps, xs2, w_gate, w_up, l):
    wspec = pl.BlockSpec((None, None, D_MODEL, TF), lambda f, i, eff, ex, va, fi: (l, ex[i], 0, f))
    return pl.pallas_call(
        _expert_up_body,
        grid_spec=pltpu.PrefetchScalarGridSpec(
            num_scalar_prefetch=4, grid=(D_EXPERT // TF, N_ET),
            in_specs=[pl.BlockSpec((TME * XP_S, LANES), lambda f, i, eff, ex, va, fi: (eff[i], 0)),
                      wspec, wspec],
            out_specs=pl.BlockSpec((TME, TF), lambda f, i, eff, ex, va, fi: (i, f)),
            scratch_shapes=[pltpu.VMEM((D_MODEL, TF), BF16), pltpu.VMEM((D_MODEL, TF), BF16)]),
        out_shape=jax.ShapeDtypeStruct((M_SORT, D_EXPERT), BF16),
        compiler_params=_cparams(2), name="expert_up")(*maps, xs2, w_gate, w_up)


def _expert_down_body(eff_ref, exp_ref, valid_ref, first_ref, h_ref, wd_ref, o_ref, wd_s):
    i = pl.program_id(0)

    @pl.when(first_ref[i] == 1)
    def _():
        wd_s[...] = wd_ref[...].astype(BF16)

    @pl.when(valid_ref[i] == 1)
    def _():
        y = jnp.dot(h_ref[...], wd_s[...], preferred_element_type=F32)
        for s in range(Y_S):
            o_ref[pl.ds(s, TME, stride=Y_S), :] = y[:, s * LANES:(s + 1) * LANES]

    @pl.when(valid_ref[i] == 0)
    def _():
        o_ref[...] = jnp.zeros(o_ref.shape, F32)


def _expert_down(maps, hs, w_down, l):
    return pl.pallas_call(
        _expert_down_body,
        grid_spec=pltpu.PrefetchScalarGridSpec(
            num_scalar_prefetch=4, grid=(N_ET,),
            in_specs=[pl.BlockSpec((TME, D_EXPERT), lambda i, eff, ex, va, fi: (eff[i], 0)),
                      pl.BlockSpec((None, None, D_EXPERT, D_MODEL),
                                   lambda i, eff, ex, va, fi: (l, ex[i], 0, 0))],
            out_specs=pl.BlockSpec((TME * Y_S, LANES), lambda i, eff, ex, va, fi: (i, 0)),
            scratch_shapes=[pltpu.VMEM((D_EXPERT, D_MODEL), BF16)]),
        out_shape=jax.ShapeDtypeStruct((M_SORT * Y_S, LANES), F32),
        compiler_params=_cparams(1), name="expert_down")(*maps, hs, w_down)


def _combine_body(p_ref, w_ref, x_ref, y_hbm, g_ref, b_ref, of_ref, ob_ref, s0, s1, sem0, sem1):
    i = pl.program_id(0)
    base = i * TM

    def issue(r, carry):
        t = base + r
        dst = pl.ds(pl.multiple_of(r * Y_S, Y_S), Y_S)
        pltpu.make_async_copy(y_hbm.at[p_ref[t]], s0.at[dst], sem0).start()
        pltpu.make_async_copy(y_hbm.at[p_ref[M_ALL + t]], s1.at[dst], sem1).start()
        return carry

    lax.fori_loop(0, TM, issue, 0)
    pltpu.make_async_copy(s0, s0, sem0).wait()
    pltpu.make_async_copy(s1, s1, sem1).wait()

    for c in range(TM // CHUNK):
        rows = pl.ds(c * CHUNK, CHUNK)
        y0 = jnp.concatenate([s0[pl.ds(c * CHUNK * Y_S + s, CHUNK, stride=Y_S), :] for s in range(Y_S)], axis=1)
        y1 = jnp.concatenate([s1[pl.ds(c * CHUNK * Y_S + s, CHUNK, stride=Y_S), :] for s in range(Y_S)], axis=1)
        w = w_ref[rows, :]
        z = ALPHA * x_ref[rows, :] + w[:, 0:1] * y0 + w[:, 1:2] * y1
        z = _ln(z, g_ref[...], b_ref[...])
        of_ref[rows, :] = z
        ob_ref[rows, :] = z.astype(BF16)


def _combine(p_flat, w_tok, x2f, ys3, g, b):
    row = pl.BlockSpec((TM, D_MODEL), lambda i, p: (i, 0))
    return pl.pallas_call(
        _combine_body,
        grid_spec=pltpu.PrefetchScalarGridSpec(
            num_scalar_prefetch=1, grid=(N_MT,),
            in_specs=[pl.BlockSpec((TM, 2), lambda i, p: (i, 0)), row,
                      pl.BlockSpec(memory_space=pl.ANY),
                      pl.BlockSpec((1, D_MODEL), lambda i, p: (0, 0)),
                      pl.BlockSpec((1, D_MODEL), lambda i, p: (0, 0))],
            out_specs=[row, row],
            scratch_shapes=[pltpu.VMEM((TM * Y_S, LANES), F32), pltpu.VMEM((TM * Y_S, LANES), F32),
                            pltpu.SemaphoreType.DMA(()), pltpu.SemaphoreType.DMA(())]),
        out_shape=[jax.ShapeDtypeStruct((M_ALL, D_MODEL), F32),
                   jax.ShapeDtypeStruct((M_ALL, D_MODEL), BF16)],
        compiler_params=_cparams(1), name="moe_combine")(p_flat, w_tok, x2f, ys3, g, b)


def _row(v):
    return v.reshape(1, -1)


def _tile_maps(cnt):
    tiles = (cnt + TME - 1) // TME
    ends = jnp.cumsum(tiles)
    total = ends[-1]
    ids = jnp.arange(N_ET, dtype=I32)
    eff = jnp.minimum(ids, total - 1)
    expert = jnp.minimum(jnp.searchsorted(ends, eff, side="right"), N_EXPERTS - 1).astype(I32)
    valid = (ids < total).astype(I32)
    prev = jnp.concatenate([jnp.full((1,), -1, I32), expert[:-1]])
    first = jnp.logical_and(valid == 1, expert != prev).astype(I32)
    start = ((ends - tiles) * TME).astype(I32)
    return (eff.astype(I32), expert, valid, first), start


def kernel(x_prompt, x_sample, mem_prompt, cache_conv, cache_mem_k, cache_mem_v, ln_in_g, ln_in_b, w_in, b_in, gm_ln_g, gm_ln_b, gm_w_s, gm_b_s, w_a, conv_w, conv_b, cv_ln_g, cv_ln_b, w_b, w_o, ln1_g, ln1_b, w_q, w_k, w_v, w_xo, ln2_g, ln2_b, w_router, b_router, w_gate, w_up, w_down, ln3_g, ln3_b):
    x_all = jnp.concatenate([x_prompt.reshape(M_PROMPT, D_MODEL), x_sample.reshape(DEC_BATCH, D_MODEL)], axis=0)
    xf, xb = _ln_in(x_all, _row(ln_in_g), _row(ln_in_b))
    mem_b = mem_prompt.reshape(BATCH * MEM_LEN, D_MODEL).astype(BF16)
    wr_t = w_router.T
    br = b_router.reshape(N_EXPERTS, 1)

    gelu = jax.nn.gelu
    o_u, o_v, o_c, o_g = 0, GM_WIDTH, 2 * GM_WIDTH, 2 * GM_WIDTH + 2 * CV_WIDTH
    mk_l, mv_l, cp_l, cs_l, vs_l = [], [], [], [], []
    for l in range(DEPTH):
        bi = b_in[l]
        u = _proj(xb, w_in[l, :, o_u:o_v].astype(BF16), _row(bi[o_u:o_v]),
                  lambda a: (gelu(a),), [BF16], tn=GM_WIDTH, name="proj_u")[0]
        v = _proj(xb, w_in[l, :, o_v:o_c].astype(BF16), _row(bi[o_v:o_c]),
                  lambda a, g, b: (_ln(gelu(a), g, b),), [F32], tn=GM_WIDTH,
                  params=(_row(gm_ln_g[l]), _row(gm_ln_b[l])), name="proj_v")[0]
        c = _glu(xb, w_in[l, :, o_c:o_c + CV_WIDTH].astype(BF16), w_in[l, :, o_c + CV_WIDTH:o_g].astype(BF16),
                 _row(bi[o_c:o_c + CV_WIDTH]), _row(bi[o_c + CV_WIDTH:o_g]))
        gates = _proj(xb, w_in[l, :, o_g:].astype(BF16), _row(bi[o_g:]),
                      lambda a: (jax.nn.sigmoid(a),), [BF16], tn=1024, name="proj_gates")[0]
        sg = _gate(u, v, gm_w_s[l], gm_b_s[l].T)
        cw, cb = conv_w[l], _row(conv_b[l])
        cg, cbe = _row(cv_ln_g[l]), _row(cv_ln_b[l])
        h = _conv(c, cw, cb, cg, cbe)
        h_s = _conv_sample(cache_conv, l, c, cw, cb, cg, cbe)
        h = lax.dynamic_update_slice(h, h_s, (M_PROMPT, 0))
        merged = _merge(sg, h, w_a[l].astype(BF16), w_b[l].astype(BF16), gates)
        x1f, x1b = _proj_ln(merged, w_o[l].astype(BF16), xf, _row(ln1_g[l]), _row(ln1_b[l]), "wo_ln")

        zero_b = jnp.zeros((1, D_MODEL), F32)
        q = _proj(x1b, w_q[l].astype(BF16), zero_b, lambda a: (a,), [BF16], tn=1024, name="proj_q")[0]
        mk_f, mk_b = _kv(mem_b, w_k[l].astype(BF16))
        mv_f, mv_b = _kv(mem_b, w_v[l].astype(BF16))
        o_p = _attn_prompt(q, mk_b, mv_b)
        q_s = q[M_PROMPT:].astype(F32).reshape(DEC_BATCH, X_HEADS, X_HEAD_DIM)
        o_s = _attn_sample(q_s, cache_mem_k, cache_mem_v, l)
        o_all = jnp.concatenate([o_p, o_s.reshape(DEC_BATCH, D_MODEL).astype(BF16)], axis=0)
        x2f, xp, logits_t = _proj_ln_route(o_all, w_xo[l].astype(BF16), x1f, _row(ln2_g[l]), _row(ln2_b[l]), wr_t)

        w_tok, p_tok, cnt_f = _router(logits_t, br)
        cnt = cnt_f[:, 0].astype(I32)
        maps, start = _tile_maps(cnt)
        p_flat = p_tok.reshape(2 * M_ALL)
        xs = _scatter(p_flat, cnt, start, xp.reshape(M_ALL, XP_S, LANES))
        hs = _expert_up(maps, xs.reshape(M_SORT * XP_S, LANES), w_gate, w_up, l)
        ys = _expert_down(maps, hs, w_down, l)
        xf, xb = _combine(p_flat, w_tok.T, x2f, ys.reshape(M_SORT, Y_S, LANES), _row(ln3_g[l]), _row(ln3_b[l]))

        mk_l.append(mk_f.reshape(BATCH, MEM_LEN, X_HEADS, X_HEAD_DIM))
        mv_l.append(mv_f.reshape(BATCH, MEM_LEN, X_HEADS, X_HEAD_DIM))
        cp_l.append(c[:M_PROMPT].reshape(BATCH, SEQ, CV_WIDTH)[:, SEQ - (CONV_K - 1):])
        cs_l.append(jnp.concatenate([cache_conv[l][:, 1:], c[M_PROMPT:][:, None, :]], axis=1))
        vs_l.append(v[M_PROMPT:].reshape(DEC_BATCH, 1, GM_WIDTH))

    return (xf[:M_PROMPT].reshape(BATCH, SEQ, D_MODEL), xf[M_PROMPT:].reshape(DEC_BATCH, 1, D_MODEL),
            jnp.stack(mk_l), jnp.stack(mv_l), jnp.stack(cp_l), jnp.stack(cs_l), jnp.stack(vs_l))
```

```python
import functools

import jax
import jax.numpy as jnp
from jax import lax
from jax.experimental import pallas as pl
from jax.experimental.pallas import tpu as pltpu

F32 = jnp.float32
BF16 = jnp.bfloat16
U32 = jnp.uint32
I32 = jnp.int32

D_MODEL = 2048
BATCH = 4
SEQ = 2048
DEPTH = 4
DEC_BATCH = 128
CHUNK = 128
GM_GROUPS = 4
GM_WIDTH = D_MODEL // 2
GM_GW = GM_WIDTH // GM_GROUPS
CV_WIDTH = D_MODEL // 2
CONV_K = 31
MEM_LEN = 256
X_HEADS = 4
X_HEAD_DIM = D_MODEL // X_HEADS
N_EXPERTS = 16
N_GROUPS = 4
EPG = N_EXPERTS // N_GROUPS
D_EXPERT = D_MODEL // 2
ALPHA = (2 * DEPTH) ** 0.25
LN_EPS = 1e-5

M_PROMPT = BATCH * SEQ
M_ALL = M_PROMPT + DEC_BATCH
TM = 640
TM_WIDE = 1664
TM_MID = 832
N_MT = M_ALL // TM
CH = 320
LANES = 128
SUBLANES = 8
HALO = 32
TME = 512
N_ET = (2 * M_ALL + N_EXPERTS * (TME - 1) + TME - 1) // TME
M_SORT = N_ET * TME
XP_S = (D_MODEL // 2) // LANES
Y_S = D_MODEL // LANES
TF = 512
VMEM_LIMIT = 56 * 1024 * 1024


def _cparams(n_grid):
    return pltpu.CompilerParams(dimension_semantics=("arbitrary",) * n_grid,
                                vmem_limit_bytes=VMEM_LIMIT)


def _const_spec(shape, single=False):
    nd = len(shape)
    if single:
        return pl.BlockSpec(shape, lambda *_: (0,) * nd, pipeline_mode=pl.Buffered(1))
    return pl.BlockSpec(shape, lambda *_: (0,) * nd)


def _ln(x, g, b):
    mu = jnp.mean(x, axis=-1, keepdims=True)
    xc = x - mu
    var = jnp.mean(xc * xc, axis=-1, keepdims=True)
    return xc * lax.rsqrt(var + LN_EPS) * g + b


def _ln_in_body(x_ref, g_ref, b_ref, of_ref, ob_ref):
    for c in range(TM // CHUNK):
        rows = pl.ds(c * CHUNK, CHUNK)
        y = _ln(x_ref[rows, :], g_ref[...], b_ref[...])
        of_ref[rows, :] = y
        ob_ref[rows, :] = y.astype(BF16)


def _ln_in(x, g, b):
    row = pl.BlockSpec((TM, D_MODEL), lambda i: (i, 0))
    return pl.pallas_call(
        _ln_in_body, grid=(N_MT,),
        in_specs=[row, _const_spec((1, D_MODEL)), _const_spec((1, D_MODEL))],
        out_specs=[row, row],
        out_shape=[jax.ShapeDtypeStruct((M_ALL, D_MODEL), F32),
                   jax.ShapeDtypeStruct((M_ALL, D_MODEL), BF16)],
        compiler_params=_cparams(1), name="ln_in")(x, g, b)


def _proj_body(n_par, epilogue, x_ref, w_ref, b_ref, *rest):
    pars = rest[:n_par]
    outs = rest[n_par:-1]
    w_s = rest[-1]

    @pl.when(pl.program_id(1) == 0)
    def _():
        w_s[...] = w_ref[...].astype(BF16)

    acc = jnp.dot(x_ref[...], w_s[...], preferred_element_type=F32) + b_ref[...]
    res = epilogue(acc, *[p[...] for p in pars])
    for o, r in zip(outs, res):
        o[...] = r.astype(o.dtype)


def _proj(x, w, l, col0, n, b, epilogue, out_dtypes, *, tn, tm, params=(), name):
    m, k = x.shape
    grid = (n // tn, m // tm)
    cb0 = col0 // tn
    colp = pl.BlockSpec((1, tn), lambda j, i: (0, j))
    out_spec = pl.BlockSpec((tm, tn), lambda j, i: (i, j))
    return pl.pallas_call(
        functools.partial(_proj_body, len(params), epilogue), grid=grid,
        in_specs=[pl.BlockSpec((tm, k), lambda j, i: (i, 0)),
                  pl.BlockSpec((None, k, tn), lambda j, i: (l, 0, cb0 + j)),
                  colp] + [colp] * len(params),
        out_specs=[out_spec] * len(out_dtypes),
        out_shape=[jax.ShapeDtypeStruct((m, n), d) for d in out_dtypes],
        scratch_shapes=[pltpu.VMEM((k, tn), BF16)],
        compiler_params=_cparams(2), name=name)(x, w, b, *params)


def _glu_body(x_ref, wa_ref, wb_ref, ba_ref, bb_ref, o_ref, wa_s, wb_s):
    @pl.when(pl.program_id(1) == 0)
    def _():
        wa_s[...] = wa_ref[...].astype(BF16)
        wb_s[...] = wb_ref[...].astype(BF16)

    x = x_ref[...]
    a = jnp.dot(x, wa_s[...], preferred_element_type=F32) + ba_ref[...]
    g = jnp.dot(x, wb_s[...], preferred_element_type=F32) + bb_ref[...]
    o_ref[...] = a * jax.nn.sigmoid(g)


def _glu(x, w, l, col0, ba, bb):
    tn, tm = 512, TM_MID
    nb = CV_WIDTH // tn
    cb0 = col0 // tn
    bspec = pl.BlockSpec((1, tn), lambda j, i: (0, j))
    return pl.pallas_call(
        _glu_body, grid=(nb, M_ALL // tm),
        in_specs=[pl.BlockSpec((tm, D_MODEL), lambda j, i: (i, 0)),
                  pl.BlockSpec((None, D_MODEL, tn), lambda j, i: (l, 0, cb0 + j)),
                  pl.BlockSpec((None, D_MODEL, tn), lambda j, i: (l, 0, cb0 + nb + j)),
                  bspec, bspec],
        out_specs=pl.BlockSpec((tm, tn), lambda j, i: (i, j)),
        out_shape=jax.ShapeDtypeStruct((M_ALL, CV_WIDTH), F32),
        scratch_shapes=[pltpu.VMEM((D_MODEL, tn), BF16), pltpu.VMEM((D_MODEL, tn), BF16)],
        compiler_params=_cparams(2), name="glu")(x, w, w, ba, bb)


def _gate_body(u_ref, v_ref, ws_ref, bs_ref, o_ref):
    i = pl.program_id(0)
    r = lax.broadcasted_iota(I32, (CHUNK, CHUNK), 0)
    c = lax.broadcasted_iota(I32, (CHUNK, CHUNK), 1)
    n_chunk = TM // CHUNK
    for j in range(n_chunk):
        rows = pl.ds(j * CHUNK, CHUNK)
        for g in range(GM_GROUPS):
            cols = pl.ds(g * GM_GW, GM_GW)
            w = ws_ref[g]
            bias = bs_ref[:, g:g + 1]
            wm = jnp.where(r >= c, w, 0.0)
            if j == n_chunk - 1:
                is_s = i == N_MT - 1
                wm = jnp.where(is_s, jnp.where(r == c, w[0:1, 0:1], 0.0), wm)
                bias = jnp.where(is_s, bias[0:1, :], bias)
            mixed = jnp.dot(wm.astype(BF16), v_ref[rows, cols].astype(BF16),
                            preferred_element_type=F32) + bias
            o_ref[rows, cols] = (u_ref[rows, cols].astype(F32) * mixed).astype(BF16)


def _gate(u, v, w_s, b_s_t):
    row = pl.BlockSpec((TM, GM_WIDTH), lambda i: (i, 0))
    return pl.pallas_call(
        _gate_body, grid=(N_MT,),
        in_specs=[row, row, _const_spec((GM_GROUPS, CHUNK, CHUNK)), _const_spec((CHUNK, GM_GROUPS))],
        out_specs=row,
        out_shape=jax.ShapeDtypeStruct((M_ALL, GM_WIDTH), BF16),
        compiler_params=_cparams(1), name="spatial_gate")(u, v, w_s, b_s_t)


def _conv_body(c_ref, halo_ref, w_ref, b_ref, g_ref, be_ref, o_ref, ext_ref, acc_ref):
    i = pl.program_id(0)
    ext_ref[0:HALO, :] = halo_ref[...]
    ext_ref[HALO:, :] = c_ref[...]
    n_chunk = TM // CHUNK
    n_cb = CV_WIDTH // LANES
    row_id = lax.broadcasted_iota(I32, (CHUNK + HALO, LANES), 0)

    for j in range(n_chunk):
        opens = ((i * n_chunk + j) % (SEQ // CHUNK)) == 0

        def chan_block(cb, carry):
            cols = pl.ds(pl.multiple_of(cb * LANES, LANES), LANES)
            acc = jnp.zeros((CHUNK, LANES), F32) + b_ref[:, cols]
            first_tap = HALO - (CONV_K - 1)
            for sub in range(SUBLANES):
                steps = [a for a in range(HALO // SUBLANES + 1)
                         if first_tap <= a * SUBLANES + sub <= HALO]
                n_rows = steps[-1] * SUBLANES + CHUNK
                win = ext_ref[pl.ds(j * CHUNK + sub, n_rows), cols]
                win = jnp.where(jnp.logical_and(opens, row_id[:n_rows] < HALO - sub), 0.0, win)
                for a in steps:
                    k = a * SUBLANES + sub - first_tap
                    acc = acc + w_ref[k:k + 1, cols] * win[a * SUBLANES:a * SUBLANES + CHUNK, :]
            acc_ref[:, cols] = acc
            return carry

        lax.fori_loop(0, n_cb, chan_block, 0)
        h = _ln(acc_ref[...], g_ref[...], be_ref[...])
        o_ref[pl.ds(j * CHUNK, CHUNK), :] = (h * jax.nn.sigmoid(h)).astype(BF16)


def _conv(c, w, b, g, be):
    blocks_per_tile = TM // HALO
    return pl.pallas_call(
        _conv_body, grid=(N_MT,),
        in_specs=[pl.BlockSpec((TM, CV_WIDTH), lambda i: (i, 0)),
                  pl.BlockSpec((HALO, CV_WIDTH), lambda i: (jnp.maximum(i * blocks_per_tile - 1, 0), 0)),
                  _const_spec((CONV_K, CV_WIDTH)), _const_spec((1, CV_WIDTH)),
                  _const_spec((1, CV_WIDTH)), _const_spec((1, CV_WIDTH))],
        out_specs=pl.BlockSpec((TM, CV_WIDTH), lambda i: (i, 0)),
        out_shape=jax.ShapeDtypeStruct((M_ALL, CV_WIDTH), BF16),
        scratch_shapes=[pltpu.VMEM((TM + HALO, CV_WIDTH), F32), pltpu.VMEM((CHUNK, CV_WIDTH), F32)],
        compiler_params=_cparams(1), name="conv_prompt")(c, c, w, b, g, be)


CONV_TB = 16


def _conv_s_body(cache_ref, c_ref, w_ref, b_ref, g_ref, be_ref, o_ref):
    hist = jnp.sum(cache_ref[...] * w_ref[0:CONV_K - 1, :][None], axis=1)
    acc = hist + c_ref[...] * w_ref[CONV_K - 1:CONV_K, :] + b_ref[...]
    h = _ln(acc, g_ref[...], be_ref[...])
    o_ref[...] = (h * jax.nn.sigmoid(h)).astype(BF16)


def _conv_sample(cache, l, c, w, b, g, be):
    first = M_PROMPT // CONV_TB
    return pl.pallas_call(
        _conv_s_body, grid=(DEC_BATCH // CONV_TB,),
        in_specs=[pl.BlockSpec((None, CONV_TB, CONV_K - 1, CV_WIDTH), lambda i: (l, i, 0, 0)),
                  pl.BlockSpec((CONV_TB, CV_WIDTH), lambda i: (first + i, 0)),
                  _const_spec((CONV_K, CV_WIDTH)), _const_spec((1, CV_WIDTH)),
                  _const_spec((1, CV_WIDTH)), _const_spec((1, CV_WIDTH))],
        out_specs=pl.BlockSpec((CONV_TB, CV_WIDTH), lambda i: (i, 0)),
        out_shape=jax.ShapeDtypeStruct((DEC_BATCH, CV_WIDTH), BF16),
        compiler_params=_cparams(1), name="conv_sample")(cache, c, w, b, g, be)


def _merge_body(sg_ref, h_ref, wa_ref, wb_ref, ga_ref, gb_ref, o_ref, wa_s, wb_s):
    @pl.when(pl.program_id(1) == 0)
    def _():
        wa_s[...] = wa_ref[...].astype(BF16)
        wb_s[...] = wb_ref[...].astype(BF16)

    a = jnp.dot(sg_ref[...], wa_s[...], preferred_element_type=F32)
    b = jnp.dot(h_ref[...], wb_s[...], preferred_element_type=F32)
    o_ref[...] = (ga_ref[...].astype(F32) * a + gb_ref[...].astype(F32) * b).astype(BF16)


def _merge(sg, h, wa, wb, l, gates):
    tn, tm = 512, TM_WIDE
    nb = D_MODEL // tn
    lhs = pl.BlockSpec((tm, GM_WIDTH), lambda j, i: (i, 0))
    wspec = pl.BlockSpec((None, GM_WIDTH, tn), lambda j, i: (l, 0, j))
    return pl.pallas_call(
        _merge_body, grid=(nb, M_ALL // tm),
        in_specs=[lhs, lhs, wspec, wspec,
                  pl.BlockSpec((tm, tn), lambda j, i: (i, j)),
                  pl.BlockSpec((tm, tn), lambda j, i: (i, j + nb))],
        out_specs=pl.BlockSpec((tm, tn), lambda j, i: (i, j)),
        out_shape=jax.ShapeDtypeStruct((M_ALL, D_MODEL), BF16),
        scratch_shapes=[pltpu.VMEM((GM_WIDTH, tn), BF16), pltpu.VMEM((GM_WIDTH, tn), BF16)],
        compiler_params=_cparams(2), name="merge")(sg, h, wa, wb, gates, gates)


def _proj_ln_body(a_ref, w_ref, r_ref, g_ref, b_ref, of_ref, ob_ref):
    for c in range(TM // CH):
        rows = pl.ds(c * CH, CH)
        y = jnp.dot(a_ref[rows, :], w_ref[...], preferred_element_type=F32)
        z = _ln(ALPHA * r_ref[rows, :] + y, g_ref[...], b_ref[...])
        of_ref[rows, :] = z
        ob_ref[rows, :] = z.astype(BF16)


def _proj_ln(a, w, r, g, b, name):
    row = pl.BlockSpec((TM, D_MODEL), lambda i: (i, 0))
    return pl.pallas_call(
        _proj_ln_body, grid=(N_MT,),
        in_specs=[row, _const_spec((D_MODEL, D_MODEL), single=True), row,
                  _const_spec((1, D_MODEL)), _const_spec((1, D_MODEL))],
        out_specs=[row, row],
        out_shape=[jax.ShapeDtypeStruct((M_ALL, D_MODEL), F32),
                   jax.ShapeDtypeStruct((M_ALL, D_MODEL), BF16)],
        compiler_params=_cparams(1), name=name)(a, w, r, g, b)


def _pack_pair(lo, hi):
    lo_b = pltpu.bitcast(lo.astype(BF16).astype(F32), U32)
    hi_b = pltpu.bitcast(hi.astype(BF16).astype(F32), U32)
    return lax.shift_right_logical(lo_b, jnp.uint32(16)) | (hi_b & jnp.uint32(0xFFFF0000))


def _unpack_pair(p):
    lo = pltpu.bitcast(lax.shift_left(p, jnp.uint32(16)), F32)
    hi = pltpu.bitcast(p & jnp.uint32(0xFFFF0000), F32)
    return lo.astype(BF16), hi.astype(BF16)


def _proj_ln_route_body(a_ref, w_ref, r_ref, g_ref, b_ref, wr_ref, of_ref, xp_ref, lg_ref):
    half = D_MODEL // 2
    for c in range(TM // CH):
        rows = pl.ds(c * CH, CH)
        y = jnp.dot(a_ref[rows, :], w_ref[...], preferred_element_type=F32)
        z = _ln(ALPHA * r_ref[rows, :] + y, g_ref[...], b_ref[...])
        of_ref[rows, :] = z
        packed = _pack_pair(z[:, :half], z[:, half:])
        for s in range(XP_S):
            xp_ref[pl.ds(c * CH * XP_S + s, CH, stride=XP_S), :] = packed[:, s * LANES:(s + 1) * LANES]
    lg_ref[...] = lax.dot_general(wr_ref[...], of_ref[...], (((1,), (1,)), ((), ())),
                                  precision=lax.Precision.HIGHEST, preferred_element_type=F32)


def _proj_ln_route(a, w, r, g, b, wr_t):
    row = pl.BlockSpec((TM, D_MODEL), lambda i: (i, 0))
    return pl.pallas_call(
        _proj_ln_route_body, grid=(N_MT,),
        in_specs=[row, _const_spec((D_MODEL, D_MODEL), single=True), row,
                  _const_spec((1, D_MODEL)), _const_spec((1, D_MODEL)),
                  _const_spec((N_EXPERTS, D_MODEL))],
        out_specs=[row, pl.BlockSpec((TM * XP_S, LANES), lambda i: (i, 0)),
                   pl.BlockSpec((N_EXPERTS, TM), lambda i: (0, i))],
        out_shape=[jax.ShapeDtypeStruct((M_ALL, D_MODEL), F32),
                   jax.ShapeDtypeStruct((M_ALL * XP_S, LANES), U32),
                   jax.ShapeDtypeStruct((N_EXPERTS, M_ALL), F32)],
        compiler_params=_cparams(1), name="xo_ln_route")(a, w, r, g, b, wr_t)


def _kv_body(x_ref, w_ref, of_ref, ob_ref):
    y = jnp.dot(x_ref[...], w_ref[...].astype(BF16), preferred_element_type=F32)
    of_ref[...] = y
    ob_ref[...] = y.astype(BF16)


def _kv(mem_b, w, l):
    tn = 1024
    m = mem_b.shape[0]
    out = pl.BlockSpec((m, tn), lambda j: (0, j))
    return pl.pallas_call(
        _kv_body, grid=(D_MODEL // tn,),
        in_specs=[_const_spec((m, D_MODEL)), pl.BlockSpec((None, D_MODEL, tn), lambda j: (l, 0, j))],
        out_specs=[out, out],
        out_shape=[jax.ShapeDtypeStruct((m, D_MODEL), F32), jax.ShapeDtypeStruct((m, D_MODEL), BF16)],
        compiler_params=_cparams(1), name="mem_kv")(mem_b, w)


TQ = 512


def _attn_body(q_ref, k_ref, v_ref, o_ref):
    scale = X_HEAD_DIM ** -0.5
    n_prompt_tiles = M_PROMPT // TQ

    @pl.when(pl.program_id(0) < n_prompt_tiles)
    def _():
        for h in range(X_HEADS):
            cols = pl.ds(h * X_HEAD_DIM, X_HEAD_DIM)
            s = lax.dot_general(q_ref[:, cols], k_ref[:, cols], (((1,), (1,)), ((), ())),
                                preferred_element_type=F32) * scale
            s = s - jnp.max(s, axis=-1, keepdims=True)
            e = jnp.exp(s)
            p = e / jnp.sum(e, axis=-1, keepdims=True)
            o = jnp.dot(p.astype(BF16), v_ref[:, cols], preferred_element_type=F32)
            o_ref[:, cols] = o.astype(BF16)

    @pl.when(pl.program_id(0) >= n_prompt_tiles)
    def _():
        o_ref[...] = jnp.zeros(o_ref.shape, BF16)


def _attn_prompt(q, k, v):
    per_b = SEQ // TQ
    last = M_PROMPT // TQ - 1
    kv = pl.BlockSpec((MEM_LEN, D_MODEL), lambda i: (jnp.minimum(i, last) // per_b, 0))
    return pl.pallas_call(
        _attn_body, grid=(pl.cdiv(M_ALL, TQ),),
        in_specs=[pl.BlockSpec((TQ, D_MODEL), lambda i: (jnp.minimum(i, last), 0)), kv, kv],
        out_specs=pl.BlockSpec((TQ, D_MODEL), lambda i: (i, 0)),
        out_shape=jax.ShapeDtypeStruct((M_ALL, D_MODEL), BF16),
        compiler_params=_cparams(1), name="attn_prompt")(q, k, v)


ATT_TB = 2


def _attn_s_body(q_ref, k_ref, v_ref, o_ref):
    scale = X_HEAD_DIM ** -0.5
    for t in range(ATT_TB):
        q = q_ref[t].astype(F32)
        s = jnp.sum(k_ref[t] * q[None], axis=-1, keepdims=True) * scale
        s = s - jnp.max(s, axis=0, keepdims=True)
        e = jnp.exp(s)
        p = e / jnp.sum(e, axis=0, keepdims=True)
        o_ref[t] = jnp.sum(p * v_ref[t], axis=0)


def _attn_sample(q_s, cache_k, cache_v, l):
    kv = pl.BlockSpec((None, ATT_TB, MEM_LEN, X_HEADS, X_HEAD_DIM), lambda i: (l, i, 0, 0, 0))
    qo = pl.BlockSpec((ATT_TB, X_HEADS, X_HEAD_DIM), lambda i: (i, 0, 0))
    return pl.pallas_call(
        _attn_s_body, grid=(DEC_BATCH // ATT_TB,),
        in_specs=[qo, kv, kv], out_specs=qo,
        out_shape=jax.ShapeDtypeStruct((DEC_BATCH, X_HEADS, X_HEAD_DIM), F32),
        compiler_params=_cparams(1), name="attn_sample")(q_s, cache_k, cache_v)


def _router_body(lg_ref, br_ref, w_ref, p_ref, cnt_ref, sel_ref, rank_ref):
    aff = jax.nn.sigmoid(lg_ref[...])
    sel = aff + br_ref[...]
    srow = [sel[e:e + 1, :] for e in range(N_EXPERTS)]
    arow = [aff[e:e + 1, :] for e in range(N_EXPERTS)]

    scores = []
    for g in range(N_GROUPS):
        q = srow[g * EPG:(g + 1) * EPG]
        best = None
        for a in range(EPG):
            for b in range(a + 1, EPG):
                pair = q[a] + q[b]
                best = pair if best is None else jnp.maximum(best, pair)
        scores.append(best)
    top = functools.reduce(jnp.maximum, scores)
    taken = jnp.zeros_like(top, dtype=jnp.bool_)
    chosen = []
    for g in range(N_GROUPS):
        is_g = jnp.logical_and(scores[g] == top, jnp.logical_not(taken))
        taken = jnp.logical_or(taken, is_g)
        chosen.append(is_g)

    mask = []
    for g in range(N_GROUPS):
        q = srow[g * EPG:(g + 1) * EPG]
        for a in range(EPG):
            rank = jnp.zeros_like(top)
            for b in range(EPG):
                if b == a:
                    continue
                ahead = q[b] > q[a]
                if b < a:
                    ahead = jnp.logical_or(ahead, q[b] == q[a])
                rank = rank + ahead.astype(F32)
            mask.append(jnp.logical_and(chosen[g], rank < 2.0))

    wsel = [jnp.where(mask[e], arow[e], 0.0) for e in range(N_EXPERTS)]
    den = functools.reduce(lambda x, y: x + y, wsel)
    for e in range(N_EXPERTS):
        sel_ref[e:e + 1, :] = mask[e].astype(F32)

    tri = (lax.broadcasted_iota(I32, (LANES, LANES), 0) <=
           lax.broadcasted_iota(I32, (LANES, LANES), 1)).astype(BF16)
    run = jnp.zeros((N_EXPERTS, 1), F32)
    for blk in range(M_ALL // LANES):
        cols = pl.ds(blk * LANES, LANES)
        m = sel_ref[:, cols]
        incl = jnp.dot(m.astype(BF16), tri, preferred_element_type=F32)
        rank_ref[:, cols] = run + incl - m
        run = run + incl[:, LANES - 1:LANES]
    padded = jnp.floor((run + (TME - 1)) * (1.0 / TME)) * TME
    low = (lax.broadcasted_iota(I32, (N_EXPERTS, N_EXPERTS), 0) >
           lax.broadcasted_iota(I32, (N_EXPERTS, N_EXPERTS), 1)).astype(BF16)
    start = jnp.dot(low, jnp.broadcast_to(padded, (N_EXPERTS, LANES)).astype(BF16),
                    preferred_element_type=F32)
    cnt_ref[...] = jnp.broadcast_to(run, (N_EXPERTS, LANES))

    seen = jnp.zeros_like(taken)
    w0 = jnp.zeros_like(top)
    w1 = jnp.zeros_like(top)
    p0 = jnp.zeros_like(top)
    p1 = jnp.zeros_like(top)
    for e in range(N_EXPERTS):
        we = wsel[e] / den
        dest = rank_ref[e:e + 1, :] + start[e:e + 1, 0:1]
        first = jnp.logical_and(mask[e], jnp.logical_not(seen))
        second = jnp.logical_and(mask[e], seen)
        seen = jnp.logical_or(seen, mask[e])
        w0 = jnp.where(first, we, w0)
        w1 = jnp.where(second, we, w1)
        p0 = jnp.where(first, dest, p0)
        p1 = jnp.where(second, dest, p1)
    w_ref[0:1, :] = w0
    w_ref[1:2, :] = w1
    p_ref[0:1, :] = p0.astype(I32)
    p_ref[1:2, :] = p1.astype(I32)


def _router(logits_t, b_router):
    return pl.pallas_call(
        _router_body, grid=(1,),
        in_specs=[_const_spec((N_EXPERTS, M_ALL)), _const_spec((N_EXPERTS, 1))],
        out_specs=[_const_spec((2, M_ALL)), _const_spec((2, M_ALL)), _const_spec((N_EXPERTS, LANES))],
        out_shape=[jax.ShapeDtypeStruct((2, M_ALL), F32), jax.ShapeDtypeStruct((2, M_ALL), I32),
                   jax.ShapeDtypeStruct((N_EXPERTS, LANES), F32)],
        scratch_shapes=[pltpu.VMEM((N_EXPERTS, M_ALL), F32), pltpu.VMEM((N_EXPERTS, M_ALL), F32)],
        compiler_params=_cparams(1), name="router")(logits_t, b_router)


def _scatter_body(p_ref, cnt_ref, start_ref, x_ref, o_hbm, zero_ref, sem, zsem):
    i = pl.program_id(0)

    def pad_copies(fn):
        for e in range(N_EXPERTS):
            n_tok = cnt_ref[e]
            n_pad = (TME - n_tok % TME) % TME
            cur = start_ref[e] + n_tok
            bit = TME // 2
            while bit >= 1:
                take = (n_pad & bit) != 0

                @pl.when(take)
                def _(cur=cur, bit=bit):
                    fn(pltpu.make_async_copy(zero_ref.at[pl.ds(0, bit)], o_hbm.at[pl.ds(cur, bit)], zsem))

                cur = cur + jnp.where(take, bit, 0)
                bit //= 2

    last = N_EXPERTS - 1
    n_used = (start_ref[last] + cnt_ref[last] + TME - 1) // TME

    def tail_copies(fn):
        def one(ti, carry):
            fn(pltpu.make_async_copy(zero_ref, o_hbm.at[pl.ds(ti * TME, TME)], zsem))
            return carry
        lax.fori_loop(n_used, N_ET, one, 0)

    @pl.when(i == 0)
    def _():
        zero_ref[...] = jnp.zeros(zero_ref.shape, U32)
        pad_copies(lambda cp: cp.start())
        tail_copies(lambda cp: cp.start())

    base = i * TM

    def issue(r, carry):
        t = base + r
        src = x_ref.at[pl.ds(pl.multiple_of(r * XP_S, XP_S), XP_S), :]
        pltpu.make_async_copy(src, o_hbm.at[p_ref[t]], sem).start()
        pltpu.make_async_copy(src, o_hbm.at[p_ref[M_ALL + t]], sem).start()
        return carry

    lax.fori_loop(0, TM, issue, 0)
    done = pltpu.make_async_copy(x_ref, x_ref, sem)
    done.wait()
    done.wait()

    @pl.when(i == N_MT - 1)
    def _():
        pad_copies(lambda cp: cp.wait())
        tail_copies(lambda cp: cp.wait())


def _scatter(p_flat, cnt, start, xp):
    return pl.pallas_call(
        _scatter_body,
        grid_spec=pltpu.PrefetchScalarGridSpec(
            num_scalar_prefetch=3, grid=(N_MT,),
            in_specs=[pl.BlockSpec((TM * XP_S, LANES), lambda i, p, c, s: (i, 0))],
            out_specs=pl.BlockSpec(memory_space=pl.ANY),
            scratch_shapes=[pltpu.VMEM((TME, XP_S, LANES), U32),
                            pltpu.SemaphoreType.DMA(()), pltpu.SemaphoreType.DMA(())]),
        out_shape=jax.ShapeDtypeStruct((M_SORT, XP_S, LANES), U32),
        compiler_params=_cparams(1), name="moe_scatter")(p_flat, cnt, start, xp)


def _expert_up_body(eff_ref, exp_ref, valid_ref, first_ref, xs_ref, wg_ref, wu_ref, o_ref,
                    wg_s, wu_s):
    i = pl.program_id(1)

    @pl.when(first_ref[i] == 1)
    def _():
        wg_s[...] = wg_ref[...].astype(BF16)
        wu_s[...] = wu_ref[...].astype(BF16)

    @pl.when(valid_ref[i] == 1)
    def _():
        packed = jnp.concatenate([xs_ref[pl.ds(s, TME, stride=XP_S), :] for s in range(XP_S)], axis=1)
        lo, hi = _unpack_pair(packed)
        x = jnp.concatenate([lo, hi], axis=1)
        g = jnp.dot(x, wg_s[...], preferred_element_type=F32)
        u = jnp.dot(x, wu_s[...], preferred_element_type=F32)
        o_ref[...] = (g * jax.nn.sigmoid(g) * u).astype(BF16)

    @pl.when(valid_ref[i] == 0)
    def _():
        o_ref[...] = jnp.zeros(o_ref.shape, BF16)


def _expert_up(maps, xs2, w_gate, w_up, l):
    wspec = pl.BlockSpec((None, None, D_MODEL, TF), lambda f, i, eff, ex, va, fi: (l, ex[i], 0, f))
    return pl.pallas_call(
        _expert_up_body,
        grid_spec=pltpu.PrefetchScalarGridSpec(
            num_scalar_prefetch=4, grid=(D_EXPERT // TF, N_ET),
            in_specs=[pl.BlockSpec((TME * XP_S, LANES), lambda f, i, eff, ex, va, fi: (eff[i], 0)),
                      wspec, wspec],
            out_specs=pl.BlockSpec((TME, TF), lambda f, i, eff, ex, va, fi: (i, f)),
            scratch_shapes=[pltpu.VMEM((D_MODEL, TF), BF16), pltpu.VMEM((D_MODEL, TF), BF16)]),
        out_shape=jax.ShapeDtypeStruct((M_SORT, D_EXPERT), BF16),
        compiler_params=_cparams(2), name="expert_up")(*maps, xs2, w_gate, w_up)


def _expert_down_body(eff_ref, exp_ref, valid_ref, first_ref, h_ref, wd_ref, o_ref, wd_s):
    i = pl.program_id(0)

    @pl.when(first_ref[i] == 1)
    def _():
        wd_s[...] = wd_ref[...].astype(BF16)

    @pl.when(valid_ref[i] == 1)
    def _():
        y = jnp.dot(h_ref[...], wd_s[...], preferred_element_type=F32)
        for s in range(Y_S):
            o_ref[pl.ds(s, TME, stride=Y_S), :] = y[:, s * LANES:(s + 1) * LANES]

    @pl.when(valid_ref[i] == 0)
    def _():
        o_ref[...] = jnp.zeros(o_ref.shape, F32)


def _expert_down(maps, hs, w_down, l):
    return pl.pallas_call(
        _expert_down_body,
        grid_spec=pltpu.PrefetchScalarGridSpec(
            num_scalar_prefetch=4, grid=(N_ET,),
            in_specs=[pl.BlockSpec((TME, D_EXPERT), lambda i, eff, ex, va, fi: (eff[i], 0)),
                      pl.BlockSpec((None, None, D_EXPERT, D_MODEL),
                                   lambda i, eff, ex, va, fi: (l, ex[i], 0, 0))],
            out_specs=pl.BlockSpec((TME * Y_S, LANES), lambda i, eff, ex, va, fi: (i, 0)),
            scratch_shapes=[pltpu.VMEM((D_EXPERT, D_MODEL), BF16)]),
        out_shape=jax.ShapeDtypeStruct((M_SORT * Y_S, LANES), F32),
        compiler_params=_cparams(1), name="expert_down")(*maps, hs, w_down)


def _combine_body(p_ref, w_ref, x_ref, y_hbm, g_ref, b_ref, of_ref, ob_ref, s0, s1, sem0, sem1):
    i = pl.program_id(0)
    base = i * TM

    def issue(r, carry):
        t = base + r
        dst = pl.ds(pl.multiple_of(r * Y_S, Y_S), Y_S)
        pltpu.make_async_copy(y_hbm.at[p_ref[t]], s0.at[dst], sem0).start()
        pltpu.make_async_copy(y_hbm.at[p_ref[M_ALL + t]], s1.at[dst], sem1).start()
        return carry

    lax.fori_loop(0, TM, issue, 0)
    pltpu.make_async_copy(s0, s0, sem0).wait()
    pltpu.make_async_copy(s1, s1, sem1).wait()

    for c in range(TM // CHUNK):
        rows = pl.ds(c * CHUNK, CHUNK)
        y0 = jnp.concatenate([s0[pl.ds(c * CHUNK * Y_S + s, CHUNK, stride=Y_S), :] for s in range(Y_S)], axis=1)
        y1 = jnp.concatenate([s1[pl.ds(c * CHUNK * Y_S + s, CHUNK, stride=Y_S), :] for s in range(Y_S)], axis=1)
        w = w_ref[rows, :]
        z = ALPHA * x_ref[rows, :] + w[:, 0:1] * y0 + w[:, 1:2] * y1
        z = _ln(z, g_ref[...], b_ref[...])
        of_ref[rows, :] = z
        ob_ref[rows, :] = z.astype(BF16)


def _combine(p_flat, w_tok, x2f, ys3, g, b):
    row = pl.BlockSpec((TM, D_MODEL), lambda i, p: (i, 0))
    return pl.pallas_call(
        _combine_body,
        grid_spec=pltpu.PrefetchScalarGridSpec(
            num_scalar_prefetch=1, grid=(N_MT,),
            in_specs=[pl.BlockSpec((TM, 2), lambda i, p: (i, 0)), row,
                      pl.BlockSpec(memory_space=pl.ANY),
                      pl.BlockSpec((1, D_MODEL), lambda i, p: (0, 0)),
                      pl.BlockSpec((1, D_MODEL), lambda i, p: (0, 0))],
            out_specs=[row, row],
            scratch_shapes=[pltpu.VMEM((TM * Y_S, LANES), F32), pltpu.VMEM((TM * Y_S, LANES), F32),
                            pltpu.SemaphoreType.DMA(()), pltpu.SemaphoreType.DMA(())]),
        out_shape=[jax.ShapeDtypeStruct((M_ALL, D_MODEL), F32),
                   jax.ShapeDtypeStruct((M_ALL, D_MODEL), BF16)],
        compiler_params=_cparams(1), name="moe_combine")(p_flat, w_tok, x2f, ys3, g, b)


def _row(v):
    return v.reshape(1, -1)


def _tile_maps(cnt):
    tiles = (cnt + TME - 1) // TME
    ends = jnp.cumsum(tiles)
    total = ends[-1]
    ids = jnp.arange(N_ET, dtype=I32)
    eff = jnp.minimum(ids, total - 1)
    expert = jnp.minimum(jnp.sum((eff[:, None] >= ends[None, :]).astype(I32), axis=1), N_EXPERTS - 1)
    valid = (ids < total).astype(I32)
    prev = jnp.concatenate([jnp.full((1,), -1, I32), expert[:-1]])
    first = jnp.logical_and(valid == 1, expert != prev).astype(I32)
    start = ((ends - tiles) * TME).astype(I32)
    return (eff.astype(I32), expert, valid, first), start


def kernel(x_prompt, x_sample, mem_prompt, cache_conv, cache_mem_k, cache_mem_v, ln_in_g, ln_in_b, w_in, b_in, gm_ln_g, gm_ln_b, gm_w_s, gm_b_s, w_a, conv_w, conv_b, cv_ln_g, cv_ln_b, w_b, w_o, ln1_g, ln1_b, w_q, w_k, w_v, w_xo, ln2_g, ln2_b, w_router, b_router, w_gate, w_up, w_down, ln3_g, ln3_b):
    x_all = jnp.concatenate([x_prompt.reshape(M_PROMPT, D_MODEL), x_sample.reshape(DEC_BATCH, D_MODEL)], axis=0)
    xf, xb = _ln_in(x_all, _row(ln_in_g), _row(ln_in_b))
    mem_b = mem_prompt.reshape(BATCH * MEM_LEN, D_MODEL).astype(BF16)
    wr_t = w_router.T
    br = b_router.reshape(N_EXPERTS, 1)

    gelu = jax.nn.gelu
    o_u, o_v, o_c, o_g = 0, GM_WIDTH, 2 * GM_WIDTH, 2 * GM_WIDTH + 2 * CV_WIDTH
    zero_b = jnp.zeros((1, D_MODEL), F32)
    mk_l, mv_l, cp_l, cs_l, vs_l = [], [], [], [], []
    for l in range(DEPTH):
        bi = b_in[l]
        u = _proj(xb, w_in, l, o_u, GM_WIDTH, _row(bi[o_u:o_v]),
                  lambda a: (gelu(a),), [BF16], tn=512, tm=TM_WIDE, name="proj_u")[0]
        v = _proj(xb, w_in, l, o_v, GM_WIDTH, _row(bi[o_v:o_c]),
                  lambda a, g, b: (_ln(gelu(a), g, b),), [F32], tn=GM_WIDTH, tm=TM_MID,
                  params=(_row(gm_ln_g[l]), _row(gm_ln_b[l])), name="proj_v")[0]
        c = _glu(xb, w_in, l, o_c, _row(bi[o_c:o_c + CV_WIDTH]), _row(bi[o_c + CV_WIDTH:o_g]))
        gates = _proj(xb, w_in, l, o_g, 2 * D_MODEL, _row(bi[o_g:]),
                      lambda a: (jax.nn.sigmoid(a),), [BF16], tn=512, tm=TM_WIDE, name="proj_gates")[0]
        sg = _gate(u, v, gm_w_s[l], gm_b_s[l].T)
        cw, cb = conv_w[l], _row(conv_b[l])
        cg, cbe = _row(cv_ln_g[l]), _row(cv_ln_b[l])
        h = _conv(c, cw, cb, cg, cbe)
        h_s = _conv_sample(cache_conv, l, c, cw, cb, cg, cbe)
        h = lax.dynamic_update_slice(h, h_s, (M_PROMPT, 0))
        merged = _merge(sg, h, w_a, w_b, l, gates)
        x1f, x1b = _proj_ln(merged, w_o[l].astype(BF16), xf, _row(ln1_g[l]), _row(ln1_b[l]), "wo_ln")

        q = _proj(x1b, w_q, l, 0, D_MODEL, zero_b, lambda a: (a,), [BF16], tn=512, tm=TM_WIDE, name="proj_q")[0]
        mk_f, mk_b = _kv(mem_b, w_k, l)
        mv_f, mv_b = _kv(mem_b, w_v, l)
        o_all = _attn_prompt(q, mk_b, mv_b)
        q_s = q[M_PROMPT:].astype(F32).reshape(DEC_BATCH, X_HEADS, X_HEAD_DIM)
        o_s = _attn_sample(q_s, cache_mem_k, cache_mem_v, l)
        o_all = lax.dynamic_update_slice(o_all, o_s.reshape(DEC_BATCH, D_MODEL).astype(BF16), (M_PROMPT, 0))
        x2f, xp, logits_t = _proj_ln_route(o_all, w_xo[l].astype(BF16), x1f, _row(ln2_g[l]), _row(ln2_b[l]), wr_t)

        w_tok, p_tok, cnt_f = _router(logits_t, br)
        cnt = cnt_f[:, 0].astype(I32)
        maps, start = _tile_maps(cnt)
        p_flat = p_tok.reshape(2 * M_ALL)
        xs = _scatter(p_flat, cnt, start, xp)
        hs = _expert_up(maps, xs.reshape(M_SORT * XP_S, LANES), w_gate, w_up, l)
        ys = _expert_down(maps, hs, w_down, l)
        xf, xb = _combine(p_flat, w_tok.T, x2f, ys.reshape(M_SORT, Y_S, LANES), _row(ln3_g[l]), _row(ln3_b[l]))

        mk_l.append(mk_f.reshape(BATCH, MEM_LEN, X_HEADS, X_HEAD_DIM))
        mv_l.append(mv_f.reshape(BATCH, MEM_LEN, X_HEADS, X_HEAD_DIM))
        cp_l.append(c[:M_PROMPT].reshape(BATCH, SEQ, CV_WIDTH)[:, SEQ - (CONV_K - 1):])
        cs_l.append(jnp.concatenate([cache_conv[l][:, 1:], c[M_PROMPT:][:, None, :]], axis=1))
        vs_l.append(v[M_PROMPT:].reshape(DEC_BATCH, 1, GM_WIDTH))

    return (xf[:M_PROMPT].reshape(BATCH, SEQ, D_MODEL), xf[M_PROMPT:].reshape(DEC_BATCH, 1, D_MODEL),
            jnp.stack(mk_l), jnp.stack(mv_l), jnp.stack(cp_l), jnp.stack(cs_l), jnp.stack(vs_l))
```

```python
import functools

import jax
import jax.numpy as jnp
from jax import lax
from jax.experimental import pallas as pl
from jax.experimental.pallas import tpu as pltpu

F32 = jnp.float32
BF16 = jnp.bfloat16
U32 = jnp.uint32
I32 = jnp.int32

D_MODEL = 2048
BATCH = 4
SEQ = 2048
DEPTH = 4
DEC_BATCH = 128
CHUNK = 128
GM_GROUPS = 4
GM_WIDTH = D_MODEL // 2
GM_GW = GM_WIDTH // GM_GROUPS
CV_WIDTH = D_MODEL // 2
CONV_K = 31
MEM_LEN = 256
X_HEADS = 4
X_HEAD_DIM = D_MODEL // X_HEADS
N_EXPERTS = 16
N_GROUPS = 4
EPG = N_EXPERTS // N_GROUPS
D_EXPERT = D_MODEL // 2
ALPHA = (2 * DEPTH) ** 0.25
LN_EPS = 1e-5

M_PROMPT = BATCH * SEQ
M_ALL = M_PROMPT + DEC_BATCH
TM = 640
TM_WIDE = 1664
TM_MID = 832
N_MT = M_ALL // TM
CH = 320
LANES = 128
SUBLANES = 8
HALO = 32
TME = 256
N_ET = (2 * M_ALL + N_EXPERTS * (TME - 1) + TME - 1) // TME
M_SORT = N_ET * TME
XP_S = (D_MODEL // 2) // LANES
Y_S = D_MODEL // LANES
VMEM_LIMIT = 56 * 1024 * 1024


def _cparams(n_grid):
    return pltpu.CompilerParams(dimension_semantics=("arbitrary",) * n_grid,
                                vmem_limit_bytes=VMEM_LIMIT)


def _const_spec(shape, single=False):
    nd = len(shape)
    if single:
        return pl.BlockSpec(shape, lambda *_: (0,) * nd, pipeline_mode=pl.Buffered(1))
    return pl.BlockSpec(shape, lambda *_: (0,) * nd)


def _ln(x, g, b):
    mu = jnp.mean(x, axis=-1, keepdims=True)
    xc = x - mu
    var = jnp.mean(xc * xc, axis=-1, keepdims=True)
    return xc * lax.rsqrt(var + LN_EPS) * g + b


def _ln_in_body(x_ref, g_ref, b_ref, of_ref, ob_ref):
    for c in range(TM // CHUNK):
        rows = pl.ds(c * CHUNK, CHUNK)
        y = _ln(x_ref[rows, :], g_ref[...], b_ref[...])
        of_ref[rows, :] = y
        ob_ref[rows, :] = y.astype(BF16)


def _ln_in(x, g, b):
    row = pl.BlockSpec((TM, D_MODEL), lambda i: (i, 0))
    return pl.pallas_call(
        _ln_in_body, grid=(N_MT,),
        in_specs=[row, _const_spec((1, D_MODEL)), _const_spec((1, D_MODEL))],
        out_specs=[row, row],
        out_shape=[jax.ShapeDtypeStruct((M_ALL, D_MODEL), F32),
                   jax.ShapeDtypeStruct((M_ALL, D_MODEL), BF16)],
        compiler_params=_cparams(1), name="ln_in")(x, g, b)


def _proj_body(n_par, epilogue, x_ref, w_ref, b_ref, *rest):
    pars = rest[:n_par]
    outs = rest[n_par:-1]
    w_s = rest[-1]

    @pl.when(pl.program_id(1) == 0)
    def _():
        w_s[...] = w_ref[...].astype(BF16)

    acc = jnp.dot(x_ref[...], w_s[...], preferred_element_type=F32) + b_ref[...]
    res = epilogue(acc, *[p[...] for p in pars])
    for o, r in zip(outs, res):
        o[...] = r.astype(o.dtype)


def _proj(x, w, l, col0, n, b, epilogue, out_dtypes, *, tn, tm, params=(), name):
    m, k = x.shape
    grid = (n // tn, m // tm)
    cb0 = col0 // tn
    colp = pl.BlockSpec((1, tn), lambda j, i: (0, j))
    out_spec = pl.BlockSpec((tm, tn), lambda j, i: (i, j))
    return pl.pallas_call(
        functools.partial(_proj_body, len(params), epilogue), grid=grid,
        in_specs=[pl.BlockSpec((tm, k), lambda j, i: (i, 0)),
                  pl.BlockSpec((None, k, tn), lambda j, i: (l, 0, cb0 + j)),
                  colp] + [colp] * len(params),
        out_specs=[out_spec] * len(out_dtypes),
        out_shape=[jax.ShapeDtypeStruct((m, n), d) for d in out_dtypes],
        scratch_shapes=[pltpu.VMEM((k, tn), BF16)],
        compiler_params=_cparams(2), name=name)(x, w, b, *params)


def _glu_body(x_ref, wa_ref, wb_ref, ba_ref, bb_ref, o_ref, wa_s, wb_s):
    @pl.when(pl.program_id(1) == 0)
    def _():
        wa_s[...] = wa_ref[...].astype(BF16)
        wb_s[...] = wb_ref[...].astype(BF16)

    x = x_ref[...]
    a = jnp.dot(x, wa_s[...], preferred_element_type=F32) + ba_ref[...]
    g = jnp.dot(x, wb_s[...], preferred_element_type=F32) + bb_ref[...]
    o_ref[...] = a * jax.nn.sigmoid(g)


def _glu(x, w, l, col0, ba, bb):
    tn, tm = 512, TM_MID
    nb = CV_WIDTH // tn
    cb0 = col0 // tn
    bspec = pl.BlockSpec((1, tn), lambda j, i: (0, j))
    return pl.pallas_call(
        _glu_body, grid=(nb, M_ALL // tm),
        in_specs=[pl.BlockSpec((tm, D_MODEL), lambda j, i: (i, 0)),
                  pl.BlockSpec((None, D_MODEL, tn), lambda j, i: (l, 0, cb0 + j)),
                  pl.BlockSpec((None, D_MODEL, tn), lambda j, i: (l, 0, cb0 + nb + j)),
                  bspec, bspec],
        out_specs=pl.BlockSpec((tm, tn), lambda j, i: (i, j)),
        out_shape=jax.ShapeDtypeStruct((M_ALL, CV_WIDTH), F32),
        scratch_shapes=[pltpu.VMEM((D_MODEL, tn), BF16), pltpu.VMEM((D_MODEL, tn), BF16)],
        compiler_params=_cparams(2), name="glu")(x, w, w, ba, bb)


def _gate_body(u_ref, v_ref, ws_ref, bs_ref, o_ref):
    i = pl.program_id(0)
    r = lax.broadcasted_iota(I32, (CHUNK, CHUNK), 0)
    c = lax.broadcasted_iota(I32, (CHUNK, CHUNK), 1)
    n_chunk = TM // CHUNK
    for j in range(n_chunk):
        rows = pl.ds(j * CHUNK, CHUNK)
        for g in range(GM_GROUPS):
            cols = pl.ds(g * GM_GW, GM_GW)
            w = ws_ref[g]
            bias = bs_ref[:, g:g + 1]
            wm = jnp.where(r >= c, w, 0.0)
            if j == n_chunk - 1:
                is_s = i == N_MT - 1
                wm = jnp.where(is_s, jnp.where(r == c, w[0:1, 0:1], 0.0), wm)
                bias = jnp.where(is_s, bias[0:1, :], bias)
            mixed = jnp.dot(wm.astype(BF16), v_ref[rows, cols].astype(BF16),
                            preferred_element_type=F32) + bias
            o_ref[rows, cols] = (u_ref[rows, cols].astype(F32) * mixed).astype(BF16)


def _gate(u, v, w_s, b_s_t):
    row = pl.BlockSpec((TM, GM_WIDTH), lambda i: (i, 0))
    return pl.pallas_call(
        _gate_body, grid=(N_MT,),
        in_specs=[row, row, _const_spec((GM_GROUPS, CHUNK, CHUNK)), _const_spec((CHUNK, GM_GROUPS))],
        out_specs=row,
        out_shape=jax.ShapeDtypeStruct((M_ALL, GM_WIDTH), BF16),
        compiler_params=_cparams(1), name="spatial_gate")(u, v, w_s, b_s_t)


def _conv_body(c_ref, halo_ref, w_ref, b_ref, g_ref, be_ref, o_ref, ext_ref, acc_ref):
    i = pl.program_id(0)
    ext_ref[0:HALO, :] = halo_ref[...]
    ext_ref[HALO:, :] = c_ref[...]
    n_chunk = TM // CHUNK
    n_cb = CV_WIDTH // LANES
    row_id = lax.broadcasted_iota(I32, (CHUNK + HALO, LANES), 0)

    for j in range(n_chunk):
        opens = ((i * n_chunk + j) % (SEQ // CHUNK)) == 0

        def chan_block(cb, carry):
            cols = pl.ds(pl.multiple_of(cb * LANES, LANES), LANES)
            acc = jnp.zeros((CHUNK, LANES), F32) + b_ref[:, cols]
            first_tap = HALO - (CONV_K - 1)
            for sub in range(SUBLANES):
                steps = [a for a in range(HALO // SUBLANES + 1)
                         if first_tap <= a * SUBLANES + sub <= HALO]
                n_rows = steps[-1] * SUBLANES + CHUNK
                win = ext_ref[pl.ds(j * CHUNK + sub, n_rows), cols]
                win = jnp.where(jnp.logical_and(opens, row_id[:n_rows] < HALO - sub), 0.0, win)
                for a in steps:
                    k = a * SUBLANES + sub - first_tap
                    acc = acc + w_ref[k:k + 1, cols] * win[a * SUBLANES:a * SUBLANES + CHUNK, :]
            acc_ref[:, cols] = acc
            return carry

        lax.fori_loop(0, n_cb, chan_block, 0)
        h = _ln(acc_ref[...], g_ref[...], be_ref[...])
        o_ref[pl.ds(j * CHUNK, CHUNK), :] = (h * jax.nn.sigmoid(h)).astype(BF16)


def _conv(c, w, b, g, be):
    blocks_per_tile = TM // HALO
    return pl.pallas_call(
        _conv_body, grid=(N_MT,),
        in_specs=[pl.BlockSpec((TM, CV_WIDTH), lambda i: (i, 0)),
                  pl.BlockSpec((HALO, CV_WIDTH), lambda i: (jnp.maximum(i * blocks_per_tile - 1, 0), 0)),
                  _const_spec((CONV_K, CV_WIDTH)), _const_spec((1, CV_WIDTH)),
                  _const_spec((1, CV_WIDTH)), _const_spec((1, CV_WIDTH))],
        out_specs=pl.BlockSpec((TM, CV_WIDTH), lambda i: (i, 0)),
        out_shape=jax.ShapeDtypeStruct((M_ALL, CV_WIDTH), BF16),
        scratch_shapes=[pltpu.VMEM((TM + HALO, CV_WIDTH), F32), pltpu.VMEM((CHUNK, CV_WIDTH), F32)],
        compiler_params=_cparams(1), name="conv_prompt")(c, c, w, b, g, be)


CONV_TB = 16


def _conv_s_body(cache_ref, c_ref, w_ref, b_ref, g_ref, be_ref, o_ref):
    hist = jnp.sum(cache_ref[...] * w_ref[0:CONV_K - 1, :][None], axis=1)
    acc = hist + c_ref[...] * w_ref[CONV_K - 1:CONV_K, :] + b_ref[...]
    h = _ln(acc, g_ref[...], be_ref[...])
    o_ref[...] = (h * jax.nn.sigmoid(h)).astype(BF16)


def _conv_sample(cache, l, c, w, b, g, be):
    first = M_PROMPT // CONV_TB
    return pl.pallas_call(
        _conv_s_body, grid=(DEC_BATCH // CONV_TB,),
        in_specs=[pl.BlockSpec((None, CONV_TB, CONV_K - 1, CV_WIDTH), lambda i: (l, i, 0, 0)),
                  pl.BlockSpec((CONV_TB, CV_WIDTH), lambda i: (first + i, 0)),
                  _const_spec((CONV_K, CV_WIDTH)), _const_spec((1, CV_WIDTH)),
                  _const_spec((1, CV_WIDTH)), _const_spec((1, CV_WIDTH))],
        out_specs=pl.BlockSpec((CONV_TB, CV_WIDTH), lambda i: (i, 0)),
        out_shape=jax.ShapeDtypeStruct((DEC_BATCH, CV_WIDTH), BF16),
        compiler_params=_cparams(1), name="conv_sample")(cache, c, w, b, g, be)


def _merge_body(sg_ref, h_ref, wa_ref, wb_ref, ga_ref, gb_ref, o_ref, wa_s, wb_s):
    @pl.when(pl.program_id(1) == 0)
    def _():
        wa_s[...] = wa_ref[...].astype(BF16)
        wb_s[...] = wb_ref[...].astype(BF16)

    a = jnp.dot(sg_ref[...], wa_s[...], preferred_element_type=F32)
    b = jnp.dot(h_ref[...], wb_s[...], preferred_element_type=F32)
    o_ref[...] = (ga_ref[...].astype(F32) * a + gb_ref[...].astype(F32) * b).astype(BF16)


def _merge(sg, h, wa, wb, l, gates):
    tn, tm = 512, TM_WIDE
    nb = D_MODEL // tn
    lhs = pl.BlockSpec((tm, GM_WIDTH), lambda j, i: (i, 0))
    wspec = pl.BlockSpec((None, GM_WIDTH, tn), lambda j, i: (l, 0, j))
    return pl.pallas_call(
        _merge_body, grid=(nb, M_ALL // tm),
        in_specs=[lhs, lhs, wspec, wspec,
                  pl.BlockSpec((tm, tn), lambda j, i: (i, j)),
                  pl.BlockSpec((tm, tn), lambda j, i: (i, j + nb))],
        out_specs=pl.BlockSpec((tm, tn), lambda j, i: (i, j)),
        out_shape=jax.ShapeDtypeStruct((M_ALL, D_MODEL), BF16),
        scratch_shapes=[pltpu.VMEM((GM_WIDTH, tn), BF16), pltpu.VMEM((GM_WIDTH, tn), BF16)],
        compiler_params=_cparams(2), name="merge")(sg, h, wa, wb, gates, gates)


def _proj_ln_body(a_ref, w_ref, r_ref, g_ref, b_ref, of_ref, ob_ref):
    for c in range(TM // CH):
        rows = pl.ds(c * CH, CH)
        y = jnp.dot(a_ref[rows, :], w_ref[...], preferred_element_type=F32)
        z = _ln(ALPHA * r_ref[rows, :] + y, g_ref[...], b_ref[...])
        of_ref[rows, :] = z
        ob_ref[rows, :] = z.astype(BF16)


def _proj_ln(a, w, r, g, b, name):
    row = pl.BlockSpec((TM, D_MODEL), lambda i: (i, 0))
    return pl.pallas_call(
        _proj_ln_body, grid=(N_MT,),
        in_specs=[row, _const_spec((D_MODEL, D_MODEL), single=True), row,
                  _const_spec((1, D_MODEL)), _const_spec((1, D_MODEL))],
        out_specs=[row, row],
        out_shape=[jax.ShapeDtypeStruct((M_ALL, D_MODEL), F32),
                   jax.ShapeDtypeStruct((M_ALL, D_MODEL), BF16)],
        compiler_params=_cparams(1), name=name)(a, w, r, g, b)


def _pack_pair(lo, hi):
    lo_b = pltpu.bitcast(lo.astype(BF16).astype(F32), U32)
    hi_b = pltpu.bitcast(hi.astype(BF16).astype(F32), U32)
    return lax.shift_right_logical(lo_b, jnp.uint32(16)) | (hi_b & jnp.uint32(0xFFFF0000))


def _unpack_pair(p):
    lo = pltpu.bitcast(lax.shift_left(p, jnp.uint32(16)), F32)
    hi = pltpu.bitcast(p & jnp.uint32(0xFFFF0000), F32)
    return lo.astype(BF16), hi.astype(BF16)


def _proj_ln_route_body(a_ref, w_ref, r_ref, g_ref, b_ref, wr_ref, of_ref, xp_ref, lg_ref):
    half = D_MODEL // 2
    for c in range(TM // CH):
        rows = pl.ds(c * CH, CH)
        y = jnp.dot(a_ref[rows, :], w_ref[...], preferred_element_type=F32)
        z = _ln(ALPHA * r_ref[rows, :] + y, g_ref[...], b_ref[...])
        of_ref[rows, :] = z
        packed = _pack_pair(z[:, :half], z[:, half:])
        for s in range(XP_S):
            xp_ref[pl.ds(c * CH * XP_S + s, CH, stride=XP_S), :] = packed[:, s * LANES:(s + 1) * LANES]
    nt = (((1,), (1,)), ((), ()))
    wr = wr_ref[...]
    wr_hi = wr.astype(BF16)
    wr_lo = (wr - wr_hi.astype(F32)).astype(BF16)
    z = of_ref[...]
    z_hi = z.astype(BF16)
    z_lo = (z - z_hi.astype(F32)).astype(BF16)
    both = lax.dot_general(jnp.concatenate([wr_hi, wr_lo], axis=0), z_hi, nt, preferred_element_type=F32)
    tail = lax.dot_general(wr_hi, z_lo, nt, preferred_element_type=F32)
    lg_ref[...] = both[:N_EXPERTS] + both[N_EXPERTS:] + tail


def _proj_ln_route(a, w, r, g, b, wr_t):
    row = pl.BlockSpec((TM, D_MODEL), lambda i: (i, 0))
    return pl.pallas_call(
        _proj_ln_route_body, grid=(N_MT,),
        in_specs=[row, _const_spec((D_MODEL, D_MODEL), single=True), row,
                  _const_spec((1, D_MODEL)), _const_spec((1, D_MODEL)),
                  _const_spec((N_EXPERTS, D_MODEL))],
        out_specs=[row, pl.BlockSpec((TM * XP_S, LANES), lambda i: (i, 0)),
                   pl.BlockSpec((N_EXPERTS, TM), lambda i: (0, i))],
        out_shape=[jax.ShapeDtypeStruct((M_ALL, D_MODEL), F32),
                   jax.ShapeDtypeStruct((M_ALL * XP_S, LANES), U32),
                   jax.ShapeDtypeStruct((N_EXPERTS, M_ALL), F32)],
        compiler_params=_cparams(1), name="xo_ln_route")(a, w, r, g, b, wr_t)


def _kv_body(x_ref, w_ref, of_ref, ob_ref):
    y = jnp.dot(x_ref[...], w_ref[...].astype(BF16), preferred_element_type=F32)
    of_ref[...] = y
    ob_ref[...] = y.astype(BF16)


def _kv(mem_b, w, l):
    tn = 1024
    m = mem_b.shape[0]
    out = pl.BlockSpec((m, tn), lambda j: (0, j))
    return pl.pallas_call(
        _kv_body, grid=(D_MODEL // tn,),
        in_specs=[_const_spec((m, D_MODEL)), pl.BlockSpec((None, D_MODEL, tn), lambda j: (l, 0, j))],
        out_specs=[out, out],
        out_shape=[jax.ShapeDtypeStruct((m, D_MODEL), F32), jax.ShapeDtypeStruct((m, D_MODEL), BF16)],
        compiler_params=_cparams(1), name="mem_kv")(mem_b, w)


TQ = 512


def _attn_body(q_ref, k_ref, v_ref, o_ref):
    scale = X_HEAD_DIM ** -0.5
    n_prompt_tiles = M_PROMPT // TQ

    @pl.when(pl.program_id(0) < n_prompt_tiles)
    def _():
        for h in range(X_HEADS):
            cols = pl.ds(h * X_HEAD_DIM, X_HEAD_DIM)
            s = lax.dot_general(q_ref[:, cols], k_ref[:, cols], (((1,), (1,)), ((), ())),
                                preferred_element_type=F32) * scale
            s = s - jnp.max(s, axis=-1, keepdims=True)
            e = jnp.exp(s)
            p = e / jnp.sum(e, axis=-1, keepdims=True)
            o = jnp.dot(p.astype(BF16), v_ref[:, cols], preferred_element_type=F32)
            o_ref[:, cols] = o.astype(BF16)

    @pl.when(pl.program_id(0) >= n_prompt_tiles)
    def _():
        o_ref[...] = jnp.zeros(o_ref.shape, BF16)


def _attn_prompt(q, k, v):
    per_b = SEQ // TQ
    last = M_PROMPT // TQ - 1
    kv = pl.BlockSpec((MEM_LEN, D_MODEL), lambda i: (jnp.minimum(i, last) // per_b, 0))
    return pl.pallas_call(
        _attn_body, grid=(pl.cdiv(M_ALL, TQ),),
        in_specs=[pl.BlockSpec((TQ, D_MODEL), lambda i: (jnp.minimum(i, last), 0)), kv, kv],
        out_specs=pl.BlockSpec((TQ, D_MODEL), lambda i: (i, 0)),
        out_shape=jax.ShapeDtypeStruct((M_ALL, D_MODEL), BF16),
        compiler_params=_cparams(1), name="attn_prompt")(q, k, v)


ATT_TB = 2


def _attn_s_body(q_ref, k_ref, v_ref, o_ref):
    scale = X_HEAD_DIM ** -0.5
    for t in range(ATT_TB):
        q = q_ref[t].astype(F32)
        s = jnp.sum(k_ref[t] * q[None], axis=-1, keepdims=True) * scale
        s = s - jnp.max(s, axis=0, keepdims=True)
        e = jnp.exp(s)
        p = e / jnp.sum(e, axis=0, keepdims=True)
        o_ref[t] = jnp.sum(p * v_ref[t], axis=0)


def _attn_sample(q_s, cache_k, cache_v, l):
    kv = pl.BlockSpec((None, ATT_TB, MEM_LEN, X_HEADS, X_HEAD_DIM), lambda i: (l, i, 0, 0, 0))
    qo = pl.BlockSpec((ATT_TB, X_HEADS, X_HEAD_DIM), lambda i: (i, 0, 0))
    return pl.pallas_call(
        _attn_s_body, grid=(DEC_BATCH // ATT_TB,),
        in_specs=[qo, kv, kv], out_specs=qo,
        out_shape=jax.ShapeDtypeStruct((DEC_BATCH, X_HEADS, X_HEAD_DIM), F32),
        compiler_params=_cparams(1), name="attn_sample")(q_s, cache_k, cache_v)


def _router_body(lg_ref, br_ref, w_ref, p_ref, cnt_ref, sel_ref, rank_ref):
    aff = jax.nn.sigmoid(lg_ref[...])
    sel = aff + br_ref[...]
    srow = [sel[e:e + 1, :] for e in range(N_EXPERTS)]
    arow = [aff[e:e + 1, :] for e in range(N_EXPERTS)]

    scores = []
    for g in range(N_GROUPS):
        q = srow[g * EPG:(g + 1) * EPG]
        best = None
        for a in range(EPG):
            for b in range(a + 1, EPG):
                pair = q[a] + q[b]
                best = pair if best is None else jnp.maximum(best, pair)
        scores.append(best)
    top = functools.reduce(jnp.maximum, scores)
    taken = jnp.zeros_like(top, dtype=jnp.bool_)
    chosen = []
    for g in range(N_GROUPS):
        is_g = jnp.logical_and(scores[g] == top, jnp.logical_not(taken))
        taken = jnp.logical_or(taken, is_g)
        chosen.append(is_g)

    mask = []
    for g in range(N_GROUPS):
        q = srow[g * EPG:(g + 1) * EPG]
        for a in range(EPG):
            rank = jnp.zeros_like(top)
            for b in range(EPG):
                if b == a:
                    continue
                ahead = q[b] > q[a]
                if b < a:
                    ahead = jnp.logical_or(ahead, q[b] == q[a])
                rank = rank + ahead.astype(F32)
            mask.append(jnp.logical_and(chosen[g], rank < 2.0))

    wsel = [jnp.where(mask[e], arow[e], 0.0) for e in range(N_EXPERTS)]
    den = functools.reduce(lambda x, y: x + y, wsel)
    for e in range(N_EXPERTS):
        sel_ref[e:e + 1, :] = mask[e].astype(F32)

    tri = (lax.broadcasted_iota(I32, (LANES, LANES), 0) <=
           lax.broadcasted_iota(I32, (LANES, LANES), 1)).astype(BF16)
    run = jnp.zeros((N_EXPERTS, 1), F32)
    for blk in range(M_ALL // LANES):
        cols = pl.ds(blk * LANES, LANES)
        m = sel_ref[:, cols]
        incl = jnp.dot(m.astype(BF16), tri, preferred_element_type=F32)
        rank_ref[:, cols] = run + incl - m
        run = run + incl[:, LANES - 1:LANES]
    padded = jnp.floor((run + (TME - 1)) * (1.0 / TME)) * TME
    low = (lax.broadcasted_iota(I32, (N_EXPERTS, N_EXPERTS), 0) >
           lax.broadcasted_iota(I32, (N_EXPERTS, N_EXPERTS), 1)).astype(BF16)
    start = jnp.dot(low, jnp.broadcast_to(padded, (N_EXPERTS, LANES)).astype(BF16),
                    preferred_element_type=F32)
    cnt_ref[...] = jnp.broadcast_to(run, (N_EXPERTS, LANES))

    seen = jnp.zeros_like(taken)
    w0 = jnp.zeros_like(top)
    w1 = jnp.zeros_like(top)
    p0 = jnp.zeros_like(top)
    p1 = jnp.zeros_like(top)
    for e in range(N_EXPERTS):
        we = wsel[e] / den
        dest = rank_ref[e:e + 1, :] + start[e:e + 1, 0:1]
        first = jnp.logical_and(mask[e], jnp.logical_not(seen))
        second = jnp.logical_and(mask[e], seen)
        seen = jnp.logical_or(seen, mask[e])
        w0 = jnp.where(first, we, w0)
        w1 = jnp.where(second, we, w1)
        p0 = jnp.where(first, dest, p0)
        p1 = jnp.where(second, dest, p1)
    w_ref[0:1, :] = w0
    w_ref[1:2, :] = w1
    p_ref[0:1, :] = p0.astype(I32)
    p_ref[1:2, :] = p1.astype(I32)


def _router(logits_t, b_router):
    return pl.pallas_call(
        _router_body, grid=(1,),
        in_specs=[_const_spec((N_EXPERTS, M_ALL)), _const_spec((N_EXPERTS, 1))],
        out_specs=[_const_spec((2, M_ALL)), _const_spec((2, M_ALL)), _const_spec((N_EXPERTS, LANES))],
        out_shape=[jax.ShapeDtypeStruct((2, M_ALL), F32), jax.ShapeDtypeStruct((2, M_ALL), I32),
                   jax.ShapeDtypeStruct((N_EXPERTS, LANES), F32)],
        scratch_shapes=[pltpu.VMEM((N_EXPERTS, M_ALL), F32), pltpu.VMEM((N_EXPERTS, M_ALL), F32)],
        compiler_params=_cparams(1), name="router")(logits_t, b_router)


def _scatter_body(p_ref, cnt_ref, start_ref, x_ref, o_hbm, zero_ref, sem, zsem):
    i = pl.program_id(0)

    def pad_copies(fn):
        for e in range(N_EXPERTS):
            n_tok = cnt_ref[e]
            n_pad = (TME - n_tok % TME) % TME
            cur = start_ref[e] + n_tok
            bit = TME // 2
            while bit >= 1:
                take = (n_pad & bit) != 0

                @pl.when(take)
                def _(cur=cur, bit=bit):
                    fn(pltpu.make_async_copy(zero_ref.at[pl.ds(0, bit)], o_hbm.at[pl.ds(cur, bit)], zsem))

                cur = cur + jnp.where(take, bit, 0)
                bit //= 2

    last = N_EXPERTS - 1
    n_used = (start_ref[last] + cnt_ref[last] + TME - 1) // TME

    def tail_copies(fn):
        def one(ti, carry):
            fn(pltpu.make_async_copy(zero_ref, o_hbm.at[pl.ds(ti * TME, TME)], zsem))
            return carry
        lax.fori_loop(n_used, N_ET, one, 0)

    @pl.when(i == 0)
    def _():
        zero_ref[...] = jnp.zeros(zero_ref.shape, U32)
        pad_copies(lambda cp: cp.start())
        tail_copies(lambda cp: cp.start())

    base = i * TM

    def issue(r, carry):
        t = base + r
        src = x_ref.at[pl.ds(pl.multiple_of(r * XP_S, XP_S), XP_S), :]
        pltpu.make_async_copy(src, o_hbm.at[p_ref[t]], sem).start()
        pltpu.make_async_copy(src, o_hbm.at[p_ref[M_ALL + t]], sem).start()
        return carry

    lax.fori_loop(0, TM, issue, 0)
    done = pltpu.make_async_copy(x_ref, x_ref, sem)
    done.wait()
    done.wait()

    @pl.when(i == N_MT - 1)
    def _():
        pad_copies(lambda cp: cp.wait())
        tail_copies(lambda cp: cp.wait())


def _scatter(p_flat, cnt, start, xp):
    return pl.pallas_call(
        _scatter_body,
        grid_spec=pltpu.PrefetchScalarGridSpec(
            num_scalar_prefetch=3, grid=(N_MT,),
            in_specs=[pl.BlockSpec((TM * XP_S, LANES), lambda i, p, c, s: (i, 0))],
            out_specs=pl.BlockSpec(memory_space=pl.ANY),
            scratch_shapes=[pltpu.VMEM((TME, XP_S, LANES), U32),
                            pltpu.SemaphoreType.DMA(()), pltpu.SemaphoreType.DMA(())]),
        out_shape=jax.ShapeDtypeStruct((M_SORT, XP_S, LANES), U32),
        compiler_params=_cparams(1), name="moe_scatter")(p_flat, cnt, start, xp)


W_CHUNK_ROWS = 512
W_CHUNK_COLS = 1024
N_WCHUNK = 3 * (D_MODEL * D_EXPERT) // (W_CHUNK_ROWS * W_CHUNK_COLS)
W_RING = 4
CAST_ROWS = 128


def _experts_body(l, eff_ref, exp_ref, valid_ref, first_ref, slot_ref, nxt_ref, hasn_ref, lo_ref, hi_ref,
                  xs_ref, wg_hbm, wu_hbm, wd_hbm, o_ref, wg_b, wu_b, wd_b, stage, sems):
    i = pl.program_id(0)
    per_mat = N_WCHUNK // 3
    d_cols = D_MODEL // W_CHUNK_COLS

    def chunk_copy(e, c):
        if c < per_mat:
            src = wg_hbm.at[l, e, pl.ds(c * W_CHUNK_ROWS, W_CHUNK_ROWS), :]
        elif c < 2 * per_mat:
            src = wu_hbm.at[l, e, pl.ds((c - per_mat) * W_CHUNK_ROWS, W_CHUNK_ROWS), :]
        else:
            r, q = divmod(c - 2 * per_mat, d_cols)
            src = wd_hbm.at[l, e, pl.ds(r * W_CHUNK_ROWS, W_CHUNK_ROWS), pl.ds(q * W_CHUNK_COLS, W_CHUNK_COLS)]
        return pltpu.make_async_copy(src, stage.at[c % W_RING], sems.at[c % W_RING])

    def cast_chunk(s, c):
        if c < per_mat:
            dst, r0, c0 = wg_b, c * W_CHUNK_ROWS, 0
        elif c < 2 * per_mat:
            dst, r0, c0 = wu_b, (c - per_mat) * W_CHUNK_ROWS, 0
        else:
            r, q = divmod(c - 2 * per_mat, d_cols)
            dst, r0, c0 = wd_b, r * W_CHUNK_ROWS, q * W_CHUNK_COLS

        def piece(k, carry):
            rows = pl.multiple_of(k * CAST_ROWS, CAST_ROWS)
            dst[s, pl.ds(r0 + rows, CAST_ROWS), pl.ds(c0, W_CHUNK_COLS)] = (
                stage[c % W_RING, pl.ds(rows, CAST_ROWS), :].astype(BF16))
            return carry

        lax.fori_loop(0, W_CHUNK_ROWS // CAST_ROWS, piece, 0)

    def finish_chunk(e, s, c):
        chunk_copy(e, c).wait()
        cast_chunk(s, c)
        if c + W_RING < N_WCHUNK:
            chunk_copy(e, c + W_RING).start()

    valid = valid_ref[i] == 1
    slot = slot_ref[i]

    @pl.when(i == 0)
    def _():
        e0 = exp_ref[0]
        for c in range(W_RING):
            chunk_copy(e0, c).start()
        for c in range(N_WCHUNK):
            finish_chunk(e0, slot, c)

    stream = jnp.logical_and(valid, hasn_ref[i] == 1)
    e_next = nxt_ref[i]

    @pl.when(jnp.logical_and(stream, first_ref[i] == 1))
    def _():
        for c in range(W_RING):
            chunk_copy(e_next, c).start()

    @pl.when(valid)
    def _():
        packed = jnp.concatenate([xs_ref[pl.ds(s, TME, stride=XP_S), :] for s in range(XP_S)], axis=1)
        lo, hi = _unpack_pair(packed)
        x = jnp.concatenate([lo, hi], axis=1)
        g = jnp.dot(x, wg_b[slot], preferred_element_type=F32)
        u = jnp.dot(x, wu_b[slot], preferred_element_type=F32)
        h = (g * jax.nn.sigmoid(g) * u).astype(BF16)
        y = jnp.dot(h, wd_b[slot], preferred_element_type=F32)
        for s in range(Y_S):
            o_ref[pl.ds(s, TME, stride=Y_S), :] = y[:, s * LANES:(s + 1) * LANES]

    @pl.when(jnp.logical_not(valid))
    def _():
        o_ref[...] = jnp.zeros(o_ref.shape, F32)

    for c in range(N_WCHUNK):
        @pl.when(jnp.logical_and(stream, jnp.logical_and(lo_ref[i] <= c, c < hi_ref[i])))
        def _(c=c):
            finish_chunk(e_next, 1 - slot, c)


def _experts(maps, xs2, w_gate, w_up, w_down, l):
    n_pre = len(maps)
    tile = lambda i, *m: (m[0][i], 0)
    return pl.pallas_call(
        functools.partial(_experts_body, l),
        grid_spec=pltpu.PrefetchScalarGridSpec(
            num_scalar_prefetch=n_pre, grid=(N_ET,),
            in_specs=[pl.BlockSpec((TME * XP_S, LANES), tile),
                      pl.BlockSpec(memory_space=pl.ANY), pl.BlockSpec(memory_space=pl.ANY),
                      pl.BlockSpec(memory_space=pl.ANY)],
            out_specs=pl.BlockSpec((TME * Y_S, LANES), lambda i, *m: (i, 0)),
            scratch_shapes=[pltpu.VMEM((2, D_MODEL, D_EXPERT), BF16), pltpu.VMEM((2, D_MODEL, D_EXPERT), BF16),
                            pltpu.VMEM((2, D_EXPERT, D_MODEL), BF16),
                            pltpu.VMEM((W_RING, W_CHUNK_ROWS, W_CHUNK_COLS), F32),
                            pltpu.SemaphoreType.DMA((W_RING,))]),
        out_shape=jax.ShapeDtypeStruct((M_SORT * Y_S, LANES), F32),
        compiler_params=_cparams(1), name="experts")(*maps, xs2, w_gate, w_up, w_down)


def _combine_body(p_ref, w_ref, x_ref, y_hbm, g_ref, b_ref, of_ref, ob_ref, s0, s1, sem0, sem1):
    i = pl.program_id(0)
    cur = i % 2

    def gather_tile(tile, buf):
        base = tile * TM

        def issue(r, carry):
            t = base + r
            dst = pl.ds(pl.multiple_of(r * Y_S, Y_S), Y_S)
            pltpu.make_async_copy(y_hbm.at[p_ref[t]], s0.at[buf, dst], sem0.at[buf]).start()
            pltpu.make_async_copy(y_hbm.at[p_ref[M_ALL + t]], s1.at[buf, dst], sem1.at[buf]).start()
            return carry

        lax.fori_loop(0, TM, issue, 0)

    @pl.when(i == 0)
    def _():
        gather_tile(0, 0)

    @pl.when(i + 1 < N_MT)
    def _():
        gather_tile(i + 1, 1 - cur)

    pltpu.make_async_copy(s0.at[cur], s0.at[cur], sem0.at[cur]).wait()
    pltpu.make_async_copy(s1.at[cur], s1.at[cur], sem1.at[cur]).wait()

    for c in range(TM // CHUNK):
        rows = pl.ds(c * CHUNK, CHUNK)
        y0 = jnp.concatenate([s0[cur, pl.ds(c * CHUNK * Y_S + s, CHUNK, stride=Y_S), :] for s in range(Y_S)], axis=1)
        y1 = jnp.concatenate([s1[cur, pl.ds(c * CHUNK * Y_S + s, CHUNK, stride=Y_S), :] for s in range(Y_S)], axis=1)
        w = w_ref[rows, :]
        z = ALPHA * x_ref[rows, :] + w[:, 0:1] * y0 + w[:, 1:2] * y1
        z = _ln(z, g_ref[...], b_ref[...])
        of_ref[rows, :] = z
        ob_ref[rows, :] = z.astype(BF16)


def _combine(p_flat, w_tok, x2f, ys3, g, b):
    row = pl.BlockSpec((TM, D_MODEL), lambda i, p: (i, 0))
    return pl.pallas_call(
        _combine_body,
        grid_spec=pltpu.PrefetchScalarGridSpec(
            num_scalar_prefetch=1, grid=(N_MT,),
            in_specs=[pl.BlockSpec((TM, 2), lambda i, p: (i, 0)), row,
                      pl.BlockSpec(memory_space=pl.ANY),
                      pl.BlockSpec((1, D_MODEL), lambda i, p: (0, 0)),
                      pl.BlockSpec((1, D_MODEL), lambda i, p: (0, 0))],
            out_specs=[row, row],
            scratch_shapes=[pltpu.VMEM((2, TM * Y_S, LANES), F32), pltpu.VMEM((2, TM * Y_S, LANES), F32),
                            pltpu.SemaphoreType.DMA((2,)), pltpu.SemaphoreType.DMA((2,))]),
        out_shape=[jax.ShapeDtypeStruct((M_ALL, D_MODEL), F32),
                   jax.ShapeDtypeStruct((M_ALL, D_MODEL), BF16)],
        compiler_params=_cparams(1), name="moe_combine")(p_flat, w_tok, x2f, ys3, g, b)


def _row(v):
    return v.reshape(1, -1)


def _tile_maps(cnt):
    tiles = (cnt + TME - 1) // TME
    ends = jnp.cumsum(tiles)
    total = ends[-1]
    ids = jnp.arange(N_ET, dtype=I32)
    eff = jnp.minimum(ids, total - 1)
    expert = jnp.minimum(jnp.sum((eff[:, None] >= ends[None, :]).astype(I32), axis=1), N_EXPERTS - 1)
    valid = ids < total
    prev = jnp.concatenate([jnp.full((1,), -1, I32), expert[:-1]])
    first = jnp.logical_and(valid, expert != prev)
    start = ((ends - tiles) * TME).astype(I32)
    eids = jnp.arange(N_EXPERTS, dtype=I32)
    used = tiles > 0
    slot_e = (jnp.cumsum(used.astype(I32)) - 1) % 2
    later = jnp.logical_and(eids[None, :] > eids[:, None], used[None, :])
    nxt_e = jnp.min(jnp.where(later, eids[None, :], N_EXPERTS), axis=1)
    has_next = nxt_e[expert] < N_EXPERTS
    n_own = jnp.maximum(tiles[expert], 1)
    j_own = eff - (ends[expert] - tiles[expert])
    lo = jnp.where(valid, (N_WCHUNK * j_own) // n_own, 0)
    hi = jnp.where(valid, (N_WCHUNK * (j_own + 1)) // n_own, 0)
    maps = (eff, expert, valid, first, slot_e[expert], jnp.minimum(nxt_e[expert], N_EXPERTS - 1),
            jnp.logical_and(valid, has_next), lo, hi)
    return tuple(m.astype(I32) for m in maps), start


def kernel(x_prompt, x_sample, mem_prompt, cache_conv, cache_mem_k, cache_mem_v, ln_in_g, ln_in_b, w_in, b_in, gm_ln_g, gm_ln_b, gm_w_s, gm_b_s, w_a, conv_w, conv_b, cv_ln_g, cv_ln_b, w_b, w_o, ln1_g, ln1_b, w_q, w_k, w_v, w_xo, ln2_g, ln2_b, w_router, b_router, w_gate, w_up, w_down, ln3_g, ln3_b):
    x_all = jnp.concatenate([x_prompt.reshape(M_PROMPT, D_MODEL), x_sample.reshape(DEC_BATCH, D_MODEL)], axis=0)
    xf, xb = _ln_in(x_all, _row(ln_in_g), _row(ln_in_b))
    mem_b = mem_prompt.reshape(BATCH * MEM_LEN, D_MODEL).astype(BF16)
    wr_t = w_router.T
    br = b_router.reshape(N_EXPERTS, 1)

    gelu = jax.nn.gelu
    o_u, o_v, o_c, o_g = 0, GM_WIDTH, 2 * GM_WIDTH, 2 * GM_WIDTH + 2 * CV_WIDTH
    zero_b = jnp.zeros((1, D_MODEL), F32)
    mk_l, mv_l, cp_l, cs_l, vs_l = [], [], [], [], []
    for l in range(DEPTH):
        bi = b_in[l]
        u = _proj(xb, w_in, l, o_u, GM_WIDTH, _row(bi[o_u:o_v]),
                  lambda a: (gelu(a),), [BF16], tn=512, tm=TM_WIDE, name="proj_u")[0]
        v = _proj(xb, w_in, l, o_v, GM_WIDTH, _row(bi[o_v:o_c]),
                  lambda a, g, b: (_ln(gelu(a), g, b),), [F32], tn=GM_WIDTH, tm=TM_MID,
                  params=(_row(gm_ln_g[l]), _row(gm_ln_b[l])), name="proj_v")[0]
        c = _glu(xb, w_in, l, o_c, _row(bi[o_c:o_c + CV_WIDTH]), _row(bi[o_c + CV_WIDTH:o_g]))
        gates = _proj(xb, w_in, l, o_g, 2 * D_MODEL, _row(bi[o_g:]),
                      lambda a: (jax.nn.sigmoid(a),), [BF16], tn=512, tm=TM_WIDE, name="proj_gates")[0]
        sg = _gate(u, v, gm_w_s[l], gm_b_s[l].T)
        cw, cb = conv_w[l], _row(conv_b[l])
        cg, cbe = _row(cv_ln_g[l]), _row(cv_ln_b[l])
        h = _conv(c, cw, cb, cg, cbe)
        h_s = _conv_sample(cache_conv, l, c, cw, cb, cg, cbe)
        h = lax.dynamic_update_slice(h, h_s, (M_PROMPT, 0))
        merged = _merge(sg, h, w_a, w_b, l, gates)
        x1f, x1b = _proj_ln(merged, w_o[l].astype(BF16), xf, _row(ln1_g[l]), _row(ln1_b[l]), "wo_ln")

        q = _proj(x1b, w_q, l, 0, D_MODEL, zero_b, lambda a: (a,), [BF16], tn=512, tm=TM_WIDE, name="proj_q")[0]
        mk_f, mk_b = _kv(mem_b, w_k, l)
        mv_f, mv_b = _kv(mem_b, w_v, l)
        o_all = _attn_prompt(q, mk_b, mv_b)
        q_s = q[M_PROMPT:].astype(F32).reshape(DEC_BATCH, X_HEADS, X_HEAD_DIM)
        o_s = _attn_sample(q_s, cache_mem_k, cache_mem_v, l)
        o_all = lax.dynamic_update_slice(o_all, o_s.reshape(DEC_BATCH, D_MODEL).astype(BF16), (M_PROMPT, 0))
        x2f, xp, logits_t = _proj_ln_route(o_all, w_xo[l].astype(BF16), x1f, _row(ln2_g[l]), _row(ln2_b[l]), wr_t)

        w_tok, p_tok, cnt_f = _router(logits_t, br)
        cnt = cnt_f[:, 0].astype(I32)
        maps, start = _tile_maps(cnt)
        p_flat = p_tok.reshape(2 * M_ALL)
        xs = _scatter(p_flat, cnt, start, xp)
        ys = _experts(maps, xs.reshape(M_SORT * XP_S, LANES), w_gate, w_up, w_down, l)
        xf, xb = _combine(p_flat, w_tok.T, x2f, ys.reshape(M_SORT, Y_S, LANES), _row(ln3_g[l]), _row(ln3_b[l]))

        mk_l.append(mk_f.reshape(BATCH, MEM_LEN, X_HEADS, X_HEAD_DIM))
        mv_l.append(mv_f.reshape(BATCH, MEM_LEN, X_HEADS, X_HEAD_DIM))
        cp_l.append(c[:M_PROMPT].reshape(BATCH, SEQ, CV_WIDTH)[:, SEQ - (CONV_K - 1):])
        cs_l.append(jnp.concatenate([cache_conv[l][:, 1:], c[M_PROMPT:][:, None, :]], axis=1))
        vs_l.append(v[M_PROMPT:].reshape(DEC_BATCH, 1, GM_WIDTH))

    return (xf[:M_PROMPT].reshape(BATCH, SEQ, D_MODEL), xf[M_PROMPT:].reshape(DEC_BATCH, 1, D_MODEL),
            jnp.stack(mk_l), jnp.stack(mv_l), jnp.stack(cp_l), jnp.stack(cs_l), jnp.stack(vs_l))
```

```python
import functools

import jax
import jax.numpy as jnp
from jax import lax
from jax.experimental import pallas as pl
from jax.experimental.pallas import tpu as pltpu

F32 = jnp.float32
BF16 = jnp.bfloat16
U32 = jnp.uint32
I32 = jnp.int32

D_MODEL = 2048
BATCH = 4
SEQ = 2048
DEPTH = 4
DEC_BATCH = 128
CHUNK = 128
GM_GROUPS = 4
GM_WIDTH = D_MODEL // 2
GM_GW = GM_WIDTH // GM_GROUPS
CV_WIDTH = D_MODEL // 2
CONV_K = 31
MEM_LEN = 256
X_HEADS = 4
X_HEAD_DIM = D_MODEL // X_HEADS
N_EXPERTS = 16
N_GROUPS = 4
EPG = N_EXPERTS // N_GROUPS
D_EXPERT = D_MODEL // 2
ALPHA = (2 * DEPTH) ** 0.25
LN_EPS = 1e-5

M_PROMPT = BATCH * SEQ
M_ALL = M_PROMPT + DEC_BATCH
TM = 640
TM_WIDE = 1664
TM_MID = 832
N_MT = M_ALL // TM
CH = 320
LANES = 128
SUBLANES = 8
HALO = 32
TME = 256
N_ET = (2 * M_ALL + N_EXPERTS * (TME - 1) + TME - 1) // TME
M_SORT = N_ET * TME
XP_S = (D_MODEL // 2) // LANES
VMEM_LIMIT = 56 * 1024 * 1024


def _cparams(n_grid):
    return pltpu.CompilerParams(dimension_semantics=("arbitrary",) * n_grid,
                                vmem_limit_bytes=VMEM_LIMIT)


def _const_spec(shape, single=False):
    nd = len(shape)
    if single:
        return pl.BlockSpec(shape, lambda *_: (0,) * nd, pipeline_mode=pl.Buffered(1))
    return pl.BlockSpec(shape, lambda *_: (0,) * nd)


def _ln(x, g, b):
    mu = jnp.mean(x, axis=-1, keepdims=True)
    xc = x - mu
    var = jnp.mean(xc * xc, axis=-1, keepdims=True)
    return xc * lax.rsqrt(var + LN_EPS) * g + b


TLN = 512


def _ln_in_body(x_ref, g_ref, b_ref, of_ref, ob_ref):
    @pl.when(pl.program_id(0) < M_PROMPT // TLN)
    def _():
        for c in range(TLN // CHUNK):
            rows = pl.ds(c * CHUNK, CHUNK)
            y = _ln(x_ref[rows, :], g_ref[...], b_ref[...])
            of_ref[rows, :] = y
            ob_ref[rows, :] = y.astype(BF16)

    @pl.when(pl.program_id(0) >= M_PROMPT // TLN)
    def _():
        of_ref[...] = jnp.zeros(of_ref.shape, F32)
        ob_ref[...] = jnp.zeros(ob_ref.shape, BF16)


def _ln_in_s_body(x_ref, g_ref, b_ref, pf_ref, pb_ref, of_ref, ob_ref):
    y = _ln(x_ref[...], g_ref[...], b_ref[...])
    of_ref[...] = y
    ob_ref[...] = y.astype(BF16)


def _ln_in(x_p, x_s, g, b):
    last = M_PROMPT // TLN - 1
    row = pl.BlockSpec((TLN, D_MODEL), lambda i: (i, 0))
    shapes = [jax.ShapeDtypeStruct((M_ALL, D_MODEL), F32), jax.ShapeDtypeStruct((M_ALL, D_MODEL), BF16)]
    xf, xb = pl.pallas_call(
        _ln_in_body, grid=(pl.cdiv(M_ALL, TLN),),
        in_specs=[pl.BlockSpec((TLN, D_MODEL), lambda i: (jnp.minimum(i, last), 0)),
                  _const_spec((1, D_MODEL)), _const_spec((1, D_MODEL))],
        out_specs=[row, row], out_shape=shapes,
        compiler_params=_cparams(1), name="ln_in")(x_p, g, b)
    tail = pl.BlockSpec((DEC_BATCH, D_MODEL), lambda i: (M_PROMPT // DEC_BATCH, 0))
    return pl.pallas_call(
        _ln_in_s_body, grid=(1,),
        in_specs=[_const_spec((DEC_BATCH, D_MODEL)), _const_spec((1, D_MODEL)), _const_spec((1, D_MODEL)),
                  pl.BlockSpec(memory_space=pl.ANY), pl.BlockSpec(memory_space=pl.ANY)],
        out_specs=[tail, tail], out_shape=shapes, input_output_aliases={3: 0, 4: 1},
        compiler_params=_cparams(1), name="ln_in_sample")(x_s, g, b, xf, xb)


def _proj_body(n_par, epilogue, x_ref, w_ref, b_ref, *rest):
    pars = rest[:n_par]
    outs = rest[n_par:-1]
    w_s = rest[-1]

    @pl.when(pl.program_id(1) == 0)
    def _():
        w_s[...] = w_ref[...].astype(BF16)

    acc = jnp.dot(x_ref[...], w_s[...], preferred_element_type=F32) + b_ref[...]
    res = epilogue(acc, *[p[...] for p in pars])
    for o, r in zip(outs, res):
        o[...] = r.astype(o.dtype)


def _proj(x, w, l, col0, n, b, epilogue, out_dtypes, *, tn, tm, params=(), name):
    m, k = x.shape
    grid = (n // tn, m // tm)
    cb0 = col0 // tn
    colp = pl.BlockSpec((1, tn), lambda j, i: (0, j))
    out_spec = pl.BlockSpec((tm, tn), lambda j, i: (i, j))
    return pl.pallas_call(
        functools.partial(_proj_body, len(params), epilogue), grid=grid,
        in_specs=[pl.BlockSpec((tm, k), lambda j, i: (i, 0)),
                  pl.BlockSpec((None, k, tn), lambda j, i: (l, 0, cb0 + j)),
                  colp] + [colp] * len(params),
        out_specs=[out_spec] * len(out_dtypes),
        out_shape=[jax.ShapeDtypeStruct((m, n), d) for d in out_dtypes],
        scratch_shapes=[pltpu.VMEM((k, tn), BF16)],
        compiler_params=_cparams(2), name=name)(x, w, b, *params)


def _glu_body(x_ref, wa_ref, wb_ref, ba_ref, bb_ref, o_ref, wa_s, wb_s):
    @pl.when(pl.program_id(1) == 0)
    def _():
        wa_s[...] = wa_ref[...].astype(BF16)
        wb_s[...] = wb_ref[...].astype(BF16)

    x = x_ref[...]
    a = jnp.dot(x, wa_s[...], preferred_element_type=F32) + ba_ref[...]
    g = jnp.dot(x, wb_s[...], preferred_element_type=F32) + bb_ref[...]
    o_ref[...] = a * jax.nn.sigmoid(g)


def _glu(x, w, l, col0, ba, bb):
    tn, tm = 512, TM_MID
    nb = CV_WIDTH // tn
    cb0 = col0 // tn
    bspec = pl.BlockSpec((1, tn), lambda j, i: (0, j))
    return pl.pallas_call(
        _glu_body, grid=(nb, M_ALL // tm),
        in_specs=[pl.BlockSpec((tm, D_MODEL), lambda j, i: (i, 0)),
                  pl.BlockSpec((None, D_MODEL, tn), lambda j, i: (l, 0, cb0 + j)),
                  pl.BlockSpec((None, D_MODEL, tn), lambda j, i: (l, 0, cb0 + nb + j)),
                  bspec, bspec],
        out_specs=pl.BlockSpec((tm, tn), lambda j, i: (i, j)),
        out_shape=jax.ShapeDtypeStruct((M_ALL, CV_WIDTH), F32),
        scratch_shapes=[pltpu.VMEM((D_MODEL, tn), BF16), pltpu.VMEM((D_MODEL, tn), BF16)],
        compiler_params=_cparams(2), name="glu")(x, w, w, ba, bb)


def _gate_body(u_ref, v_ref, ws_ref, bs_ref, o_ref):
    i = pl.program_id(0)
    r = lax.broadcasted_iota(I32, (CHUNK, CHUNK), 0)
    c = lax.broadcasted_iota(I32, (CHUNK, CHUNK), 1)
    n_chunk = TM // CHUNK
    for j in range(n_chunk):
        rows = pl.ds(j * CHUNK, CHUNK)
        for g in range(GM_GROUPS):
            cols = pl.ds(g * GM_GW, GM_GW)
            w = ws_ref[g]
            bias = bs_ref[:, g:g + 1]
            wm = jnp.where(r >= c, w, 0.0)
            if j == n_chunk - 1:
                is_s = i == N_MT - 1
                wm = jnp.where(is_s, jnp.where(r == c, w[0:1, 0:1], 0.0), wm)
                bias = jnp.where(is_s, bias[0:1, :], bias)
            mixed = jnp.dot(wm.astype(BF16), v_ref[rows, cols].astype(BF16),
                            preferred_element_type=F32) + bias
            o_ref[rows, cols] = (u_ref[rows, cols].astype(F32) * mixed).astype(BF16)


def _gate(u, v, w_s, b_s_t):
    row = pl.BlockSpec((TM, GM_WIDTH), lambda i: (i, 0))
    return pl.pallas_call(
        _gate_body, grid=(N_MT,),
        in_specs=[row, row, _const_spec((GM_GROUPS, CHUNK, CHUNK)), _const_spec((CHUNK, GM_GROUPS))],
        out_specs=row,
        out_shape=jax.ShapeDtypeStruct((M_ALL, GM_WIDTH), BF16),
        compiler_params=_cparams(1), name="spatial_gate")(u, v, w_s, b_s_t)


def _conv_body(c_ref, halo_ref, w_ref, b_ref, g_ref, be_ref, o_ref, ext_ref, acc_ref):
    i = pl.program_id(0)
    ext_ref[0:HALO, :] = halo_ref[...]
    ext_ref[HALO:, :] = c_ref[...]
    n_chunk = TM // CHUNK
    n_cb = CV_WIDTH // LANES
    row_id = lax.broadcasted_iota(I32, (CHUNK + HALO, LANES), 0)

    for j in range(n_chunk):
        opens = ((i * n_chunk + j) % (SEQ // CHUNK)) == 0

        def chan_block(cb, carry):
            cols = pl.ds(pl.multiple_of(cb * LANES, LANES), LANES)
            acc = jnp.zeros((CHUNK, LANES), F32) + b_ref[:, cols]
            first_tap = HALO - (CONV_K - 1)
            for sub in range(SUBLANES):
                steps = [a for a in range(HALO // SUBLANES + 1)
                         if first_tap <= a * SUBLANES + sub <= HALO]
                n_rows = steps[-1] * SUBLANES + CHUNK
                win = ext_ref[pl.ds(j * CHUNK + sub, n_rows), cols]
                win = jnp.where(jnp.logical_and(opens, row_id[:n_rows] < HALO - sub), 0.0, win)
                for a in steps:
                    k = a * SUBLANES + sub - first_tap
                    acc = acc + w_ref[k:k + 1, cols] * win[a * SUBLANES:a * SUBLANES + CHUNK, :]
            acc_ref[:, cols] = acc
            return carry

        lax.fori_loop(0, n_cb, chan_block, 0)
        h = _ln(acc_ref[...], g_ref[...], be_ref[...])
        o_ref[pl.ds(j * CHUNK, CHUNK), :] = (h * jax.nn.sigmoid(h)).astype(BF16)


def _conv(c, w, b, g, be):
    blocks_per_tile = TM // HALO
    return pl.pallas_call(
        _conv_body, grid=(N_MT,),
        in_specs=[pl.BlockSpec((TM, CV_WIDTH), lambda i: (i, 0)),
                  pl.BlockSpec((HALO, CV_WIDTH), lambda i: (jnp.maximum(i * blocks_per_tile - 1, 0), 0)),
                  _const_spec((CONV_K, CV_WIDTH)), _const_spec((1, CV_WIDTH)),
                  _const_spec((1, CV_WIDTH)), _const_spec((1, CV_WIDTH))],
        out_specs=pl.BlockSpec((TM, CV_WIDTH), lambda i: (i, 0)),
        out_shape=jax.ShapeDtypeStruct((M_ALL, CV_WIDTH), BF16),
        scratch_shapes=[pltpu.VMEM((TM + HALO, CV_WIDTH), F32), pltpu.VMEM((CHUNK, CV_WIDTH), F32)],
        compiler_params=_cparams(1), name="conv_prompt")(c, c, w, b, g, be)


CONV_TB = 16


def _conv_s_body(cache_ref, c_ref, w_ref, b_ref, g_ref, be_ref, state_in, o_ref, state_ref):
    n_hist = CONV_K - 1
    hist = jnp.sum(cache_ref[...] * w_ref[0:n_hist, :][None], axis=1)
    acc = hist + c_ref[...] * w_ref[n_hist:CONV_K, :] + b_ref[...]
    h = _ln(acc, g_ref[...], be_ref[...])
    o_ref[...] = (h * jax.nn.sigmoid(h)).astype(BF16)
    state_ref[:, 0:n_hist - 1, :] = cache_ref[:, 1:n_hist, :]
    for t in range(CONV_TB):
        state_ref[t, n_hist - 1:n_hist, :] = c_ref[t:t + 1, :]


def _conv_sample(cache, l, c, w, b, g, be, states):
    first = M_PROMPT // CONV_TB
    hist = pl.BlockSpec((None, CONV_TB, CONV_K - 1, CV_WIDTH), lambda i: (l, i, 0, 0))
    return pl.pallas_call(
        _conv_s_body, grid=(DEC_BATCH // CONV_TB,),
        in_specs=[hist, pl.BlockSpec((CONV_TB, CV_WIDTH), lambda i: (first + i, 0)),
                  _const_spec((CONV_K, CV_WIDTH)), _const_spec((1, CV_WIDTH)),
                  _const_spec((1, CV_WIDTH)), _const_spec((1, CV_WIDTH)),
                  pl.BlockSpec(memory_space=pl.ANY)],
        out_specs=[pl.BlockSpec((CONV_TB, CV_WIDTH), lambda i: (i, 0)), hist],
        out_shape=[jax.ShapeDtypeStruct((DEC_BATCH, CV_WIDTH), BF16),
                   jax.ShapeDtypeStruct(states.shape, F32)],
        input_output_aliases={6: 1},
        compiler_params=_cparams(1), name="conv_sample")(cache, c, w, b, g, be, states)


def _merge_body(sg_ref, h_ref, wa_ref, wb_ref, ga_ref, gb_ref, o_ref, wa_s, wb_s):
    @pl.when(pl.program_id(1) == 0)
    def _():
        wa_s[...] = wa_ref[...].astype(BF16)
        wb_s[...] = wb_ref[...].astype(BF16)

    a = jnp.dot(sg_ref[...], wa_s[...], preferred_element_type=F32)
    b = jnp.dot(h_ref[...], wb_s[...], preferred_element_type=F32)
    o_ref[...] = (ga_ref[...].astype(F32) * a + gb_ref[...].astype(F32) * b).astype(BF16)


def _merge(sg, h, wa, wb, l, gates):
    tn, tm = 512, TM_WIDE
    nb = D_MODEL // tn
    lhs = pl.BlockSpec((tm, GM_WIDTH), lambda j, i: (i, 0))
    wspec = pl.BlockSpec((None, GM_WIDTH, tn), lambda j, i: (l, 0, j))
    return pl.pallas_call(
        _merge_body, grid=(nb, M_ALL // tm),
        in_specs=[lhs, lhs, wspec, wspec,
                  pl.BlockSpec((tm, tn), lambda j, i: (i, j)),
                  pl.BlockSpec((tm, tn), lambda j, i: (i, j + nb))],
        out_specs=pl.BlockSpec((tm, tn), lambda j, i: (i, j)),
        out_shape=jax.ShapeDtypeStruct((M_ALL, D_MODEL), BF16),
        scratch_shapes=[pltpu.VMEM((GM_WIDTH, tn), BF16), pltpu.VMEM((GM_WIDTH, tn), BF16)],
        compiler_params=_cparams(2), name="merge")(sg, h, wa, wb, gates, gates)


def _proj_ln_body(a_ref, w_ref, r_ref, g_ref, b_ref, of_ref, ob_ref):
    for c in range(TM // CH):
        rows = pl.ds(c * CH, CH)
        y = jnp.dot(a_ref[rows, :], w_ref[...], preferred_element_type=F32)
        z = _ln(ALPHA * r_ref[rows, :] + y, g_ref[...], b_ref[...])
        of_ref[rows, :] = z
        ob_ref[rows, :] = z.astype(BF16)


def _proj_ln(a, w, r, g, b, name):
    row = pl.BlockSpec((TM, D_MODEL), lambda i: (i, 0))
    return pl.pallas_call(
        _proj_ln_body, grid=(N_MT,),
        in_specs=[row, _const_spec((D_MODEL, D_MODEL), single=True), row,
                  _const_spec((1, D_MODEL)), _const_spec((1, D_MODEL))],
        out_specs=[row, row],
        out_shape=[jax.ShapeDtypeStruct((M_ALL, D_MODEL), F32),
                   jax.ShapeDtypeStruct((M_ALL, D_MODEL), BF16)],
        compiler_params=_cparams(1), name=name)(a, w, r, g, b)


def _pack_pair(lo, hi):
    lo_b = pltpu.bitcast(lo.astype(BF16).astype(F32), U32)
    hi_b = pltpu.bitcast(hi.astype(BF16).astype(F32), U32)
    return lax.shift_right_logical(lo_b, jnp.uint32(16)) | (hi_b & jnp.uint32(0xFFFF0000))


def _unpack_pair(p):
    lo = pltpu.bitcast(lax.shift_left(p, jnp.uint32(16)), F32)
    hi = pltpu.bitcast(p & jnp.uint32(0xFFFF0000), F32)
    return lo.astype(BF16), hi.astype(BF16)


def _proj_ln_route_body(a_ref, w_ref, r_ref, g_ref, b_ref, wr_ref, of_ref, xp_ref, lg_ref):
    half = D_MODEL // 2
    for c in range(TM // CH):
        rows = pl.ds(c * CH, CH)
        y = jnp.dot(a_ref[rows, :], w_ref[...], preferred_element_type=F32)
        z = _ln(ALPHA * r_ref[rows, :] + y, g_ref[...], b_ref[...])
        of_ref[rows, :] = z
        packed = _pack_pair(z[:, :half], z[:, half:])
        for s in range(XP_S):
            xp_ref[pl.ds(c * CH * XP_S + s, CH, stride=XP_S), :] = packed[:, s * LANES:(s + 1) * LANES]
    nt = (((1,), (1,)), ((), ()))
    wr = wr_ref[...]
    wr_hi = wr.astype(BF16)
    wr_lo = (wr - wr_hi.astype(F32)).astype(BF16)
    z = of_ref[...]
    z_hi = z.astype(BF16)
    z_lo = (z - z_hi.astype(F32)).astype(BF16)
    both = lax.dot_general(jnp.concatenate([wr_hi, wr_lo], axis=0), z_hi, nt, preferred_element_type=F32)
    tail = lax.dot_general(wr_hi, z_lo, nt, preferred_element_type=F32)
    lg_ref[...] = both[:N_EXPERTS] + both[N_EXPERTS:] + tail


def _proj_ln_route(a, w, r, g, b, wr_t):
    row = pl.BlockSpec((TM, D_MODEL), lambda i: (i, 0))
    return pl.pallas_call(
        _proj_ln_route_body, grid=(N_MT,),
        in_specs=[row, _const_spec((D_MODEL, D_MODEL), single=True), row,
                  _const_spec((1, D_MODEL)), _const_spec((1, D_MODEL)),
                  _const_spec((N_EXPERTS, D_MODEL))],
        out_specs=[row, pl.BlockSpec((TM * XP_S, LANES), lambda i: (i, 0)),
                   pl.BlockSpec((N_EXPERTS, TM), lambda i: (0, i))],
        out_shape=[jax.ShapeDtypeStruct((M_ALL, D_MODEL), F32),
                   jax.ShapeDtypeStruct((M_ALL * XP_S, LANES), U32),
                   jax.ShapeDtypeStruct((N_EXPERTS, M_ALL), F32)],
        compiler_params=_cparams(1), name="xo_ln_route")(a, w, r, g, b, wr_t)


def _kv_body(x_ref, w_ref, of_ref, ob_ref):
    y = jnp.dot(x_ref[...], w_ref[...].astype(BF16), preferred_element_type=F32)
    of_ref[...] = y
    ob_ref[...] = y.astype(BF16)


def _kv(mem_b, w, l):
    tn = 1024
    m = mem_b.shape[0]
    out = pl.BlockSpec((m, tn), lambda j: (0, j))
    return pl.pallas_call(
        _kv_body, grid=(D_MODEL // tn,),
        in_specs=[_const_spec((m, D_MODEL)), pl.BlockSpec((None, D_MODEL, tn), lambda j: (l, 0, j))],
        out_specs=[out, out],
        out_shape=[jax.ShapeDtypeStruct((m, D_MODEL), F32), jax.ShapeDtypeStruct((m, D_MODEL), BF16)],
        compiler_params=_cparams(1), name="mem_kv")(mem_b, w)


TQ = 512


def _attn_body(q_ref, k_ref, v_ref, o_ref):
    scale = X_HEAD_DIM ** -0.5
    n_prompt_tiles = M_PROMPT // TQ

    @pl.when(pl.program_id(0) < n_prompt_tiles)
    def _():
        for h in range(X_HEADS):
            cols = pl.ds(h * X_HEAD_DIM, X_HEAD_DIM)
            s = lax.dot_general(q_ref[:, cols], k_ref[:, cols], (((1,), (1,)), ((), ())),
                                preferred_element_type=F32) * scale
            s = s - jnp.max(s, axis=-1, keepdims=True)
            e = jnp.exp(s)
            p = e / jnp.sum(e, axis=-1, keepdims=True)
            o = jnp.dot(p.astype(BF16), v_ref[:, cols], preferred_element_type=F32)
            o_ref[:, cols] = o.astype(BF16)

    @pl.when(pl.program_id(0) >= n_prompt_tiles)
    def _():
        o_ref[...] = jnp.zeros(o_ref.shape, BF16)


def _attn_prompt(q, k, v):
    per_b = SEQ // TQ
    last = M_PROMPT // TQ - 1
    kv = pl.BlockSpec((MEM_LEN, D_MODEL), lambda i: (jnp.minimum(i, last) // per_b, 0))
    return pl.pallas_call(
        _attn_body, grid=(pl.cdiv(M_ALL, TQ),),
        in_specs=[pl.BlockSpec((TQ, D_MODEL), lambda i: (jnp.minimum(i, last), 0)), kv, kv],
        out_specs=pl.BlockSpec((TQ, D_MODEL), lambda i: (i, 0)),
        out_shape=jax.ShapeDtypeStruct((M_ALL, D_MODEL), BF16),
        compiler_params=_cparams(1), name="attn_prompt")(q, k, v)


ATT_TB = 4


def _attn_s_body(q_ref, k_ref, v_ref, o_ref):
    scale = X_HEAD_DIM ** -0.5
    for t in range(ATT_TB):
        q = q_ref[t].astype(F32)
        s = jnp.sum(k_ref[t] * q[None], axis=-1, keepdims=True) * scale
        s = s - jnp.max(s, axis=0, keepdims=True)
        e = jnp.exp(s)
        p = e / jnp.sum(e, axis=0, keepdims=True)
        o_ref[t] = jnp.sum(p * v_ref[t], axis=0)


def _attn_sample(q_s, cache_k, cache_v, l):
    kv = pl.BlockSpec((None, ATT_TB, MEM_LEN, X_HEADS, X_HEAD_DIM), lambda i: (l, i, 0, 0, 0))
    qo = pl.BlockSpec((ATT_TB, X_HEADS, X_HEAD_DIM), lambda i: (i, 0, 0))
    return pl.pallas_call(
        _attn_s_body, grid=(DEC_BATCH // ATT_TB,),
        in_specs=[qo, kv, kv], out_specs=qo,
        out_shape=jax.ShapeDtypeStruct((DEC_BATCH, X_HEADS, X_HEAD_DIM), F32),
        compiler_params=_cparams(1), name="attn_sample")(q_s, cache_k, cache_v)


def _router_body(lg_ref, br_ref, w_ref, p_ref, cnt_ref, sel_ref, rank_ref):
    aff = jax.nn.sigmoid(lg_ref[...])
    sel = aff + br_ref[...]
    srow = [sel[e:e + 1, :] for e in range(N_EXPERTS)]
    arow = [aff[e:e + 1, :] for e in range(N_EXPERTS)]

    scores = []
    for g in range(N_GROUPS):
        q = srow[g * EPG:(g + 1) * EPG]
        best = None
        for a in range(EPG):
            for b in range(a + 1, EPG):
                pair = q[a] + q[b]
                best = pair if best is None else jnp.maximum(best, pair)
        scores.append(best)
    top = functools.reduce(jnp.maximum, scores)
    taken = jnp.zeros_like(top, dtype=jnp.bool_)
    chosen = []
    for g in range(N_GROUPS):
        is_g = jnp.logical_and(scores[g] == top, jnp.logical_not(taken))
        taken = jnp.logical_or(taken, is_g)
        chosen.append(is_g)

    mask = []
    for g in range(N_GROUPS):
        q = srow[g * EPG:(g + 1) * EPG]
        for a in range(EPG):
            rank = jnp.zeros_like(top)
            for b in range(EPG):
                if b == a:
                    continue
                ahead = q[b] > q[a]
                if b < a:
                    ahead = jnp.logical_or(ahead, q[b] == q[a])
                rank = rank + ahead.astype(F32)
            mask.append(jnp.logical_and(chosen[g], rank < 2.0))

    wsel = [jnp.where(mask[e], arow[e], 0.0) for e in range(N_EXPERTS)]
    den = functools.reduce(lambda x, y: x + y, wsel)
    for e in range(N_EXPERTS):
        sel_ref[e:e + 1, :] = mask[e].astype(F32)

    tri = (lax.broadcasted_iota(I32, (LANES, LANES), 0) <=
           lax.broadcasted_iota(I32, (LANES, LANES), 1)).astype(BF16)
    run = jnp.zeros((N_EXPERTS, 1), F32)
    for blk in range(M_ALL // LANES):
        cols = pl.ds(blk * LANES, LANES)
        m = sel_ref[:, cols]
        incl = jnp.dot(m.astype(BF16), tri, preferred_element_type=F32)
        rank_ref[:, cols] = run + incl - m
        run = run + incl[:, LANES - 1:LANES]
    padded = jnp.floor((run + (TME - 1)) * (1.0 / TME)) * TME
    low = (lax.broadcasted_iota(I32, (N_EXPERTS, N_EXPERTS), 0) >
           lax.broadcasted_iota(I32, (N_EXPERTS, N_EXPERTS), 1)).astype(BF16)
    start = jnp.dot(low, jnp.broadcast_to(padded, (N_EXPERTS, LANES)).astype(BF16),
                    preferred_element_type=F32)
    cnt_ref[...] = jnp.broadcast_to(run, (N_EXPERTS, LANES))

    seen = jnp.zeros_like(taken)
    w0 = jnp.zeros_like(top)
    w1 = jnp.zeros_like(top)
    p0 = jnp.zeros_like(top)
    p1 = jnp.zeros_like(top)
    for e in range(N_EXPERTS):
        we = wsel[e] / den
        dest = rank_ref[e:e + 1, :] + start[e:e + 1, 0:1]
        first = jnp.logical_and(mask[e], jnp.logical_not(seen))
        second = jnp.logical_and(mask[e], seen)
        seen = jnp.logical_or(seen, mask[e])
        w0 = jnp.where(first, we, w0)
        w1 = jnp.where(second, we, w1)
        p0 = jnp.where(first, dest, p0)
        p1 = jnp.where(second, dest, p1)
    w_ref[0:1, :] = w0
    w_ref[1:2, :] = w1
    p_ref[0:1, :] = p0.astype(I32)
    p_ref[1:2, :] = p1.astype(I32)


def _router(logits_t, b_router):
    return pl.pallas_call(
        _router_body, grid=(1,),
        in_specs=[_const_spec((N_EXPERTS, M_ALL)), _const_spec((N_EXPERTS, 1))],
        out_specs=[_const_spec((2, M_ALL)), _const_spec((2, M_ALL)), _const_spec((N_EXPERTS, LANES))],
        out_shape=[jax.ShapeDtypeStruct((2, M_ALL), F32), jax.ShapeDtypeStruct((2, M_ALL), I32),
                   jax.ShapeDtypeStruct((N_EXPERTS, LANES), F32)],
        scratch_shapes=[pltpu.VMEM((N_EXPERTS, M_ALL), F32), pltpu.VMEM((N_EXPERTS, M_ALL), F32)],
        compiler_params=_cparams(1), name="router")(logits_t, b_router)


def _scatter_body(p_ref, cnt_ref, start_ref, x_ref, o_hbm, zero_ref, sem, zsem):
    i = pl.program_id(0)

    def pad_copies(fn):
        for e in range(N_EXPERTS):
            n_tok = cnt_ref[e]
            n_pad = (TME - n_tok % TME) % TME
            cur = start_ref[e] + n_tok
            bit = TME // 2
            while bit >= 1:
                take = (n_pad & bit) != 0

                @pl.when(take)
                def _(cur=cur, bit=bit):
                    fn(pltpu.make_async_copy(zero_ref.at[pl.ds(0, bit)], o_hbm.at[pl.ds(cur, bit)], zsem))

                cur = cur + jnp.where(take, bit, 0)
                bit //= 2

    last = N_EXPERTS - 1
    n_used = (start_ref[last] + cnt_ref[last] + TME - 1) // TME

    def tail_copies(fn):
        def one(ti, carry):
            fn(pltpu.make_async_copy(zero_ref, o_hbm.at[pl.ds(ti * TME, TME)], zsem))
            return carry
        lax.fori_loop(n_used, N_ET, one, 0)

    @pl.when(i == 0)
    def _():
        zero_ref[...] = jnp.zeros(zero_ref.shape, U32)
        pad_copies(lambda cp: cp.start())
        tail_copies(lambda cp: cp.start())

    base = i * TM

    def issue(r, carry):
        t = base + r
        src = x_ref.at[pl.ds(pl.multiple_of(r * XP_S, XP_S), XP_S), :]
        pltpu.make_async_copy(src, o_hbm.at[p_ref[t]], sem).start()
        pltpu.make_async_copy(src, o_hbm.at[p_ref[M_ALL + t]], sem).start()
        return carry

    lax.fori_loop(0, TM, issue, 0)
    done = pltpu.make_async_copy(x_ref, x_ref, sem)
    done.wait()
    done.wait()

    @pl.when(i == N_MT - 1)
    def _():
        pad_copies(lambda cp: cp.wait())
        tail_copies(lambda cp: cp.wait())


def _scatter(p_flat, cnt, start, xp):
    return pl.pallas_call(
        _scatter_body,
        grid_spec=pltpu.PrefetchScalarGridSpec(
            num_scalar_prefetch=3, grid=(N_MT,),
            in_specs=[pl.BlockSpec((TM * XP_S, LANES), lambda i, p, c, s: (i, 0))],
            out_specs=pl.BlockSpec(memory_space=pl.ANY),
            scratch_shapes=[pltpu.VMEM((TME, XP_S, LANES), U32),
                            pltpu.SemaphoreType.DMA(()), pltpu.SemaphoreType.DMA(())]),
        out_shape=jax.ShapeDtypeStruct((M_SORT, XP_S, LANES), U32),
        compiler_params=_cparams(1), name="moe_scatter")(p_flat, cnt, start, xp)


W_CHUNK_ROWS = 512
W_CHUNK_COLS = 1024
N_WCHUNK = 3 * (D_MODEL * D_EXPERT) // (W_CHUNK_ROWS * W_CHUNK_COLS)
W_RING = 4
CAST_ROWS = 128


def _experts_body(l, eff_ref, exp_ref, valid_ref, first_ref, slot_ref, nxt_ref, hasn_ref, lo_ref, hi_ref,
                  xs_ref, wg_hbm, wu_hbm, wd_hbm, o_ref, wg_b, wu_b, wd_b, stage, sems):
    i = pl.program_id(0)
    per_mat = N_WCHUNK // 3
    d_cols = D_MODEL // W_CHUNK_COLS

    def chunk_copy(e, c):
        if c < per_mat:
            src = wg_hbm.at[l, e, pl.ds(c * W_CHUNK_ROWS, W_CHUNK_ROWS), :]
        elif c < 2 * per_mat:
            src = wu_hbm.at[l, e, pl.ds((c - per_mat) * W_CHUNK_ROWS, W_CHUNK_ROWS), :]
        else:
            r, q = divmod(c - 2 * per_mat, d_cols)
            src = wd_hbm.at[l, e, pl.ds(r * W_CHUNK_ROWS, W_CHUNK_ROWS), pl.ds(q * W_CHUNK_COLS, W_CHUNK_COLS)]
        return pltpu.make_async_copy(src, stage.at[c % W_RING], sems.at[c % W_RING])

    def cast_chunk(s, c):
        if c < per_mat:
            dst, r0, c0 = wg_b, c * W_CHUNK_ROWS, 0
        elif c < 2 * per_mat:
            dst, r0, c0 = wu_b, (c - per_mat) * W_CHUNK_ROWS, 0
        else:
            r, q = divmod(c - 2 * per_mat, d_cols)
            dst, r0, c0 = wd_b, r * W_CHUNK_ROWS, q * W_CHUNK_COLS

        def piece(k, carry):
            rows = pl.multiple_of(k * CAST_ROWS, CAST_ROWS)
            dst[s, pl.ds(r0 + rows, CAST_ROWS), pl.ds(c0, W_CHUNK_COLS)] = (
                stage[c % W_RING, pl.ds(rows, CAST_ROWS), :].astype(BF16))
            return carry

        lax.fori_loop(0, W_CHUNK_ROWS // CAST_ROWS, piece, 0)

    def finish_chunk(e, s, c):
        chunk_copy(e, c).wait()
        cast_chunk(s, c)
        if c + W_RING < N_WCHUNK:
            chunk_copy(e, c + W_RING).start()

    valid = valid_ref[i] == 1
    slot = slot_ref[i]

    @pl.when(i == 0)
    def _():
        e0 = exp_ref[0]
        for c in range(W_RING):
            chunk_copy(e0, c).start()
        for c in range(N_WCHUNK):
            finish_chunk(e0, slot, c)

    stream = jnp.logical_and(valid, hasn_ref[i] == 1)
    e_next = nxt_ref[i]

    @pl.when(jnp.logical_and(stream, first_ref[i] == 1))
    def _():
        for c in range(W_RING):
            chunk_copy(e_next, c).start()

    @pl.when(valid)
    def _():
        packed = jnp.concatenate([xs_ref[pl.ds(s, TME, stride=XP_S), :] for s in range(XP_S)], axis=1)
        lo, hi = _unpack_pair(packed)
        x = jnp.concatenate([lo, hi], axis=1)
        g = jnp.dot(x, wg_b[slot], preferred_element_type=F32)
        u = jnp.dot(x, wu_b[slot], preferred_element_type=F32)
        h = (g * jax.nn.sigmoid(g) * u).astype(BF16)
        y = jnp.dot(h, wd_b[slot], preferred_element_type=F32)
        half = D_MODEL // 2
        packed_y = _pack_pair(y[:, :half], y[:, half:])
        for s in range(XP_S):
            o_ref[pl.ds(s, TME, stride=XP_S), :] = packed_y[:, s * LANES:(s + 1) * LANES]

    @pl.when(jnp.logical_not(valid))
    def _():
        o_ref[...] = jnp.zeros(o_ref.shape, U32)

    for c in range(N_WCHUNK):
        @pl.when(jnp.logical_and(stream, jnp.logical_and(lo_ref[i] <= c, c < hi_ref[i])))
        def _(c=c):
            finish_chunk(e_next, 1 - slot, c)


def _experts(maps, xs2, w_gate, w_up, w_down, l):
    n_pre = len(maps)
    tile = lambda i, *m: (m[0][i], 0)
    return pl.pallas_call(
        functools.partial(_experts_body, l),
        grid_spec=pltpu.PrefetchScalarGridSpec(
            num_scalar_prefetch=n_pre, grid=(N_ET,),
            in_specs=[pl.BlockSpec((TME * XP_S, LANES), tile),
                      pl.BlockSpec(memory_space=pl.ANY), pl.BlockSpec(memory_space=pl.ANY),
                      pl.BlockSpec(memory_space=pl.ANY)],
            out_specs=pl.BlockSpec((TME * XP_S, LANES), lambda i, *m: (i, 0)),
            scratch_shapes=[pltpu.VMEM((2, D_MODEL, D_EXPERT), BF16), pltpu.VMEM((2, D_MODEL, D_EXPERT), BF16),
                            pltpu.VMEM((2, D_EXPERT, D_MODEL), BF16),
                            pltpu.VMEM((W_RING, W_CHUNK_ROWS, W_CHUNK_COLS), F32),
                            pltpu.SemaphoreType.DMA((W_RING,))]),
        out_shape=jax.ShapeDtypeStruct((M_SORT * XP_S, LANES), U32),
        compiler_params=_cparams(1), name="experts")(*maps, xs2, w_gate, w_up, w_down)


def _unpack_rows(buf, slot, row0, n):
    p = jnp.concatenate([buf[slot, pl.ds(row0 * XP_S + s, n, stride=XP_S), :] for s in range(XP_S)], axis=1)
    lo = pltpu.bitcast(lax.shift_left(p, jnp.uint32(16)), F32)
    hi = pltpu.bitcast(p & jnp.uint32(0xFFFF0000), F32)
    return jnp.concatenate([lo, hi], axis=1)


def _combine_body(p_ref, w_ref, x_ref, y_hbm, g_ref, b_ref, of_ref, ob_ref, s0, s1, sem0, sem1):
    i = pl.program_id(0)
    cur = i % 2

    def gather_tile(tile, buf):
        base = tile * TM

        def issue(r, carry):
            t = base + r
            dst = pl.ds(pl.multiple_of(r * XP_S, XP_S), XP_S)
            pltpu.make_async_copy(y_hbm.at[p_ref[t]], s0.at[buf, dst], sem0.at[buf]).start()
            pltpu.make_async_copy(y_hbm.at[p_ref[M_ALL + t]], s1.at[buf, dst], sem1.at[buf]).start()
            return carry

        lax.fori_loop(0, TM, issue, 0)

    @pl.when(i == 0)
    def _():
        gather_tile(0, 0)

    @pl.when(i + 1 < N_MT)
    def _():
        gather_tile(i + 1, 1 - cur)

    pltpu.make_async_copy(s0.at[cur], s0.at[cur], sem0.at[cur]).wait()
    pltpu.make_async_copy(s1.at[cur], s1.at[cur], sem1.at[cur]).wait()

    for c in range(TM // CHUNK):
        rows = pl.ds(c * CHUNK, CHUNK)
        y0 = _unpack_rows(s0, cur, c * CHUNK, CHUNK)
        y1 = _unpack_rows(s1, cur, c * CHUNK, CHUNK)
        w = w_ref[rows, :]
        z = ALPHA * x_ref[rows, :] + w[:, 0:1] * y0 + w[:, 1:2] * y1
        z = _ln(z, g_ref[...], b_ref[...])
        of_ref[rows, :] = z
        ob_ref[rows, :] = z.astype(BF16)


def _combine(p_flat, w_tok, x2f, ys3, g, b):
    row = pl.BlockSpec((TM, D_MODEL), lambda i, p: (i, 0))
    return pl.pallas_call(
        _combine_body,
        grid_spec=pltpu.PrefetchScalarGridSpec(
            num_scalar_prefetch=1, grid=(N_MT,),
            in_specs=[pl.BlockSpec((TM, 2), lambda i, p: (i, 0)), row,
                      pl.BlockSpec(memory_space=pl.ANY),
                      pl.BlockSpec((1, D_MODEL), lambda i, p: (0, 0)),
                      pl.BlockSpec((1, D_MODEL), lambda i, p: (0, 0))],
            out_specs=[row, row],
            scratch_shapes=[pltpu.VMEM((2, TM * XP_S, LANES), U32), pltpu.VMEM((2, TM * XP_S, LANES), U32),
                            pltpu.SemaphoreType.DMA((2,)), pltpu.SemaphoreType.DMA((2,))]),
        out_shape=[jax.ShapeDtypeStruct((M_ALL, D_MODEL), F32),
                   jax.ShapeDtypeStruct((M_ALL, D_MODEL), BF16)],
        compiler_params=_cparams(1), name="moe_combine")(p_flat, w_tok, x2f, ys3, g, b)


def _row(v):
    return v.reshape(1, -1)


def _tile_maps(cnt):
    tiles = (cnt + TME - 1) // TME
    ends = jnp.cumsum(tiles)
    total = ends[-1]
    ids = jnp.arange(N_ET, dtype=I32)
    eff = jnp.minimum(ids, total - 1)
    expert = jnp.minimum(jnp.sum((eff[:, None] >= ends[None, :]).astype(I32), axis=1), N_EXPERTS - 1)
    valid = ids < total
    prev = jnp.concatenate([jnp.full((1,), -1, I32), expert[:-1]])
    first = jnp.logical_and(valid, expert != prev)
    start = ((ends - tiles) * TME).astype(I32)
    eids = jnp.arange(N_EXPERTS, dtype=I32)
    used = tiles > 0
    slot_e = (jnp.cumsum(used.astype(I32)) - 1) % 2
    later = jnp.logical_and(eids[None, :] > eids[:, None], used[None, :])
    nxt_e = jnp.min(jnp.where(later, eids[None, :], N_EXPERTS), axis=1)
    has_next = nxt_e[expert] < N_EXPERTS
    n_own = jnp.maximum(tiles[expert], 1)
    j_own = eff - (ends[expert] - tiles[expert])
    lo = jnp.where(valid, (N_WCHUNK * j_own) // n_own, 0)
    hi = jnp.where(valid, (N_WCHUNK * (j_own + 1)) // n_own, 0)
    maps = (eff, expert, valid, first, slot_e[expert], jnp.minimum(nxt_e[expert], N_EXPERTS - 1),
            jnp.logical_and(valid, has_next), lo, hi)
    return tuple(m.astype(I32) for m in maps), start


def kernel(x_prompt, x_sample, mem_prompt, cache_conv, cache_mem_k, cache_mem_v, ln_in_g, ln_in_b, w_in, b_in, gm_ln_g, gm_ln_b, gm_w_s, gm_b_s, w_a, conv_w, conv_b, cv_ln_g, cv_ln_b, w_b, w_o, ln1_g, ln1_b, w_q, w_k, w_v, w_xo, ln2_g, ln2_b, w_router, b_router, w_gate, w_up, w_down, ln3_g, ln3_b):
    xf, xb = _ln_in(x_prompt.reshape(M_PROMPT, D_MODEL), x_sample.reshape(DEC_BATCH, D_MODEL),
                    _row(ln_in_g), _row(ln_in_b))
    conv_states = jnp.zeros(cache_conv.shape, F32)
    mem_b = mem_prompt.reshape(BATCH * MEM_LEN, D_MODEL).astype(BF16)
    wr_t = w_router.T
    br = b_router.reshape(N_EXPERTS, 1)

    gelu = jax.nn.gelu
    o_u, o_v, o_c, o_g = 0, GM_WIDTH, 2 * GM_WIDTH, 2 * GM_WIDTH + 2 * CV_WIDTH
    zero_b = jnp.zeros((1, D_MODEL), F32)
    mk_l, mv_l, cp_l, vs_l = [], [], [], []
    for l in range(DEPTH):
        bi = b_in[l]
        u = _proj(xb, w_in, l, o_u, GM_WIDTH, _row(bi[o_u:o_v]),
                  lambda a: (gelu(a),), [BF16], tn=512, tm=TM_WIDE, name="proj_u")[0]
        v = _proj(xb, w_in, l, o_v, GM_WIDTH, _row(bi[o_v:o_c]),
                  lambda a, g, b: (_ln(gelu(a), g, b),), [F32], tn=GM_WIDTH, tm=TM_MID,
                  params=(_row(gm_ln_g[l]), _row(gm_ln_b[l])), name="proj_v")[0]
        c = _glu(xb, w_in, l, o_c, _row(bi[o_c:o_c + CV_WIDTH]), _row(bi[o_c + CV_WIDTH:o_g]))
        gates = _proj(xb, w_in, l, o_g, 2 * D_MODEL, _row(bi[o_g:]),
                      lambda a: (jax.nn.sigmoid(a),), [BF16], tn=512, tm=TM_WIDE, name="proj_gates")[0]
        sg = _gate(u, v, gm_w_s[l], gm_b_s[l].T)
        cw, cb = conv_w[l], _row(conv_b[l])
        cg, cbe = _row(cv_ln_g[l]), _row(cv_ln_b[l])
        h = _conv(c, cw, cb, cg, cbe)
        h_s, conv_states = _conv_sample(cache_conv, l, c, cw, cb, cg, cbe, conv_states)
        h = lax.dynamic_update_slice(h, h_s, (M_PROMPT, 0))
        merged = _merge(sg, h, w_a, w_b, l, gates)
        x1f, x1b = _proj_ln(merged, w_o[l].astype(BF16), xf, _row(ln1_g[l]), _row(ln1_b[l]), "wo_ln")

        q = _proj(x1b, w_q, l, 0, D_MODEL, zero_b, lambda a: (a,), [BF16], tn=512, tm=TM_WIDE, name="proj_q")[0]
        mk_f, mk_b = _kv(mem_b, w_k, l)
        mv_f, mv_b = _kv(mem_b, w_v, l)
        o_all = _attn_prompt(q, mk_b, mv_b)
        q_s = q[M_PROMPT:].astype(F32).reshape(DEC_BATCH, X_HEADS, X_HEAD_DIM)
        o_s = _attn_sample(q_s, cache_mem_k, cache_mem_v, l)
        o_all = lax.dynamic_update_slice(o_all, o_s.reshape(DEC_BATCH, D_MODEL).astype(BF16), (M_PROMPT, 0))
        x2f, xp, logits_t = _proj_ln_route(o_all, w_xo[l].astype(BF16), x1f, _row(ln2_g[l]), _row(ln2_b[l]), wr_t)

        w_tok, p_tok, cnt_f = _router(logits_t, br)
        cnt = cnt_f[:, 0].astype(I32)
        maps, start = _tile_maps(cnt)
        p_flat = p_tok.reshape(2 * M_ALL)
        xs = _scatter(p_flat, cnt, start, xp)
        ys = _experts(maps, xs.reshape(M_SORT * XP_S, LANES), w_gate, w_up, w_down, l)
        xf, xb = _combine(p_flat, w_tok.T, x2f, ys.reshape(M_SORT, XP_S, LANES), _row(ln3_g[l]), _row(ln3_b[l]))

        mk_l.append(mk_f.reshape(BATCH, MEM_LEN, X_HEADS, X_HEAD_DIM))
        mv_l.append(mv_f.reshape(BATCH, MEM_LEN, X_HEADS, X_HEAD_DIM))
        cp_l.append(c[:M_PROMPT].reshape(BATCH, SEQ, CV_WIDTH)[:, SEQ - (CONV_K - 1):])
        vs_l.append(v[M_PROMPT:].reshape(DEC_BATCH, 1, GM_WIDTH))

    return (xf[:M_PROMPT].reshape(BATCH, SEQ, D_MODEL), xf[M_PROMPT:].reshape(DEC_BATCH, 1, D_MODEL),
            jnp.stack(mk_l), jnp.stack(mv_l), jnp.stack(cp_l), conv_states, jnp.stack(vs_l))
```

```python
import functools

import jax
import jax.numpy as jnp
from jax import lax
from jax.experimental import pallas as pl
from jax.experimental.pallas import tpu as pltpu

F32 = jnp.float32
BF16 = jnp.bfloat16
U32 = jnp.uint32
I32 = jnp.int32

D_MODEL = 2048
BATCH = 4
SEQ = 2048
DEPTH = 4
DEC_BATCH = 128
CHUNK = 128
GM_GROUPS = 4
GM_WIDTH = D_MODEL // 2
GM_GW = GM_WIDTH // GM_GROUPS
CV_WIDTH = D_MODEL // 2
CONV_K = 31
MEM_LEN = 256
X_HEADS = 4
X_HEAD_DIM = D_MODEL // X_HEADS
N_EXPERTS = 16
N_GROUPS = 4
EPG = N_EXPERTS // N_GROUPS
D_EXPERT = D_MODEL // 2
ALPHA = (2 * DEPTH) ** 0.25
LN_EPS = 1e-5

M_PROMPT = BATCH * SEQ
M_ALL = M_PROMPT + DEC_BATCH
TM = 640
TM_WIDE = 1664
TM_MID = 832
N_MT = M_ALL // TM
CH = 128
LANES = 128
SUBLANES = 8
HALO = 32
TME = 256
N_ET = (2 * M_ALL + N_EXPERTS * (TME - 1) + TME - 1) // TME
M_SORT = N_ET * TME
XP_S = (D_MODEL // 2) // LANES
VMEM_LIMIT = 56 * 1024 * 1024


def _cparams(n_grid):
    return pltpu.CompilerParams(dimension_semantics=("arbitrary",) * n_grid,
                                vmem_limit_bytes=VMEM_LIMIT)


def _const_spec(shape, single=False):
    nd = len(shape)
    if single:
        return pl.BlockSpec(shape, lambda *_: (0,) * nd, pipeline_mode=pl.Buffered(1))
    return pl.BlockSpec(shape, lambda *_: (0,) * nd)


def _ln(x, g, b):
    mu = jnp.mean(x, axis=-1, keepdims=True)
    xc = x - mu
    var = jnp.mean(xc * xc, axis=-1, keepdims=True)
    return xc * lax.rsqrt(var + LN_EPS) * g + b


TLN = 512


def _ln_in_body(x_ref, g_ref, b_ref, of_ref, ob_ref):
    @pl.when(pl.program_id(0) < M_PROMPT // TLN)
    def _():
        for c in range(TLN // CHUNK):
            rows = pl.ds(c * CHUNK, CHUNK)
            y = _ln(x_ref[rows, :], g_ref[...], b_ref[...])
            of_ref[rows, :] = y
            ob_ref[rows, :] = y.astype(BF16)

    @pl.when(pl.program_id(0) >= M_PROMPT // TLN)
    def _():
        of_ref[...] = jnp.zeros(of_ref.shape, F32)
        ob_ref[...] = jnp.zeros(ob_ref.shape, BF16)


def _ln_in_s_body(x_ref, g_ref, b_ref, pf_ref, pb_ref, of_ref, ob_ref):
    y = _ln(x_ref[...], g_ref[...], b_ref[...])
    of_ref[...] = y
    ob_ref[...] = y.astype(BF16)


def _ln_in(x_p, x_s, g, b):
    last = M_PROMPT // TLN - 1
    row = pl.BlockSpec((TLN, D_MODEL), lambda i: (i, 0))
    shapes = [jax.ShapeDtypeStruct((M_ALL, D_MODEL), F32), jax.ShapeDtypeStruct((M_ALL, D_MODEL), BF16)]
    xf, xb = pl.pallas_call(
        _ln_in_body, grid=(pl.cdiv(M_ALL, TLN),),
        in_specs=[pl.BlockSpec((TLN, D_MODEL), lambda i: (jnp.minimum(i, last), 0)),
                  _const_spec((1, D_MODEL)), _const_spec((1, D_MODEL))],
        out_specs=[row, row], out_shape=shapes,
        compiler_params=_cparams(1), name="ln_in")(x_p, g, b)
    tail = pl.BlockSpec((DEC_BATCH, D_MODEL), lambda i: (M_PROMPT // DEC_BATCH, 0))
    return pl.pallas_call(
        _ln_in_s_body, grid=(1,),
        in_specs=[_const_spec((DEC_BATCH, D_MODEL)), _const_spec((1, D_MODEL)), _const_spec((1, D_MODEL)),
                  pl.BlockSpec(memory_space=pl.ANY), pl.BlockSpec(memory_space=pl.ANY)],
        out_specs=[tail, tail], out_shape=shapes, input_output_aliases={3: 0, 4: 1},
        compiler_params=_cparams(1), name="ln_in_sample")(x_s, g, b, xf, xb)


def _proj_body(n_par, epilogue, x_ref, w_ref, b_ref, *rest):
    pars = rest[:n_par]
    outs = rest[n_par:-1]
    w_s = rest[-1]

    @pl.when(pl.program_id(1) == 0)
    def _():
        w_s[...] = w_ref[...].astype(BF16)

    acc = jnp.dot(x_ref[...], w_s[...], preferred_element_type=F32) + b_ref[...]
    res = epilogue(acc, *[p[...] for p in pars])
    for o, r in zip(outs, res):
        o[...] = r.astype(o.dtype)


def _proj(x, w, l, col0, n, b, epilogue, out_dtypes, *, tn, tm, params=(), name):
    m, k = x.shape
    grid = (n // tn, m // tm)
    cb0 = col0 // tn
    colp = pl.BlockSpec((1, tn), lambda j, i: (0, j))
    out_spec = pl.BlockSpec((tm, tn), lambda j, i: (i, j))
    return pl.pallas_call(
        functools.partial(_proj_body, len(params), epilogue), grid=grid,
        in_specs=[pl.BlockSpec((tm, k), lambda j, i: (i, 0)),
                  pl.BlockSpec((None, k, tn), lambda j, i: (l, 0, cb0 + j)),
                  colp] + [colp] * len(params),
        out_specs=[out_spec] * len(out_dtypes),
        out_shape=[jax.ShapeDtypeStruct((m, n), d) for d in out_dtypes],
        scratch_shapes=[pltpu.VMEM((k, tn), BF16)],
        compiler_params=_cparams(2), name=name)(x, w, b, *params)


def _glu_body(x_ref, wa_ref, wb_ref, ba_ref, bb_ref, o_ref, wa_s, wb_s):
    @pl.when(pl.program_id(1) == 0)
    def _():
        wa_s[...] = wa_ref[...].astype(BF16)
        wb_s[...] = wb_ref[...].astype(BF16)

    x = x_ref[...]
    a = jnp.dot(x, wa_s[...], preferred_element_type=F32) + ba_ref[...]
    g = jnp.dot(x, wb_s[...], preferred_element_type=F32) + bb_ref[...]
    o_ref[...] = a * jax.nn.sigmoid(g)


def _glu(x, w, l, col0, ba, bb):
    tn, tm = 512, TM_MID
    nb = CV_WIDTH // tn
    cb0 = col0 // tn
    bspec = pl.BlockSpec((1, tn), lambda j, i: (0, j))
    return pl.pallas_call(
        _glu_body, grid=(nb, M_ALL // tm),
        in_specs=[pl.BlockSpec((tm, D_MODEL), lambda j, i: (i, 0)),
                  pl.BlockSpec((None, D_MODEL, tn), lambda j, i: (l, 0, cb0 + j)),
                  pl.BlockSpec((None, D_MODEL, tn), lambda j, i: (l, 0, cb0 + nb + j)),
                  bspec, bspec],
        out_specs=pl.BlockSpec((tm, tn), lambda j, i: (i, j)),
        out_shape=jax.ShapeDtypeStruct((M_ALL, CV_WIDTH), F32),
        scratch_shapes=[pltpu.VMEM((D_MODEL, tn), BF16), pltpu.VMEM((D_MODEL, tn), BF16)],
        compiler_params=_cparams(2), name="glu")(x, w, w, ba, bb)


def _gate_body(u_ref, v_ref, ws_ref, bs_ref, o_ref):
    i = pl.program_id(0)
    r = lax.broadcasted_iota(I32, (CHUNK, CHUNK), 0)
    c = lax.broadcasted_iota(I32, (CHUNK, CHUNK), 1)
    n_chunk = TM // CHUNK
    for j in range(n_chunk):
        rows = pl.ds(j * CHUNK, CHUNK)
        for g in range(GM_GROUPS):
            cols = pl.ds(g * GM_GW, GM_GW)
            w = ws_ref[g]
            bias = bs_ref[:, g:g + 1]
            wm = jnp.where(r >= c, w, 0.0)
            if j == n_chunk - 1:
                is_s = i == N_MT - 1
                wm = jnp.where(is_s, jnp.where(r == c, w[0:1, 0:1], 0.0), wm)
                bias = jnp.where(is_s, bias[0:1, :], bias)
            mixed = jnp.dot(wm.astype(BF16), v_ref[rows, cols].astype(BF16),
                            preferred_element_type=F32) + bias
            o_ref[rows, cols] = (u_ref[rows, cols].astype(F32) * mixed).astype(BF16)


def _gate(u, v, w_s, b_s_t):
    row = pl.BlockSpec((TM, GM_WIDTH), lambda i: (i, 0))
    return pl.pallas_call(
        _gate_body, grid=(N_MT,),
        in_specs=[row, row, _const_spec((GM_GROUPS, CHUNK, CHUNK)), _const_spec((CHUNK, GM_GROUPS))],
        out_specs=row,
        out_shape=jax.ShapeDtypeStruct((M_ALL, GM_WIDTH), BF16),
        compiler_params=_cparams(1), name="spatial_gate")(u, v, w_s, b_s_t)


def _conv_body(c_ref, halo_ref, w_ref, b_ref, g_ref, be_ref, o_ref, ext_ref, acc_ref):
    i = pl.program_id(0)
    ext_ref[0:HALO, :] = halo_ref[...]
    ext_ref[HALO:, :] = c_ref[...]
    n_chunk = TM // CHUNK
    n_cb = CV_WIDTH // LANES

    for j in range(n_chunk):
        @pl.when(((i * n_chunk + j) % (SEQ // CHUNK)) == 0)
        def _(j=j):
            ext_ref[pl.ds(j * CHUNK, HALO), :] = jnp.zeros((HALO, CV_WIDTH), F32)

        def chan_block(cb, carry):
            cols = pl.ds(pl.multiple_of(cb * LANES, LANES), LANES)
            acc = jnp.zeros((CHUNK, LANES), F32) + b_ref[:, cols]
            first_tap = HALO - (CONV_K - 1)
            for sub in range(SUBLANES):
                steps = [a for a in range(HALO // SUBLANES + 1)
                         if first_tap <= a * SUBLANES + sub <= HALO]
                n_rows = steps[-1] * SUBLANES + CHUNK
                win = ext_ref[pl.ds(j * CHUNK + sub, n_rows), cols]
                for a in steps:
                    k = a * SUBLANES + sub - first_tap
                    acc = acc + w_ref[k:k + 1, cols] * win[a * SUBLANES:a * SUBLANES + CHUNK, :]
            acc_ref[:, cols] = acc
            return carry

        lax.fori_loop(0, n_cb, chan_block, 0)
        h = _ln(acc_ref[...], g_ref[...], be_ref[...])
        o_ref[pl.ds(j * CHUNK, CHUNK), :] = (h * jax.nn.sigmoid(h)).astype(BF16)


def _conv(c, w, b, g, be):
    blocks_per_tile = TM // HALO
    return pl.pallas_call(
        _conv_body, grid=(N_MT,),
        in_specs=[pl.BlockSpec((TM, CV_WIDTH), lambda i: (i, 0)),
                  pl.BlockSpec((HALO, CV_WIDTH), lambda i: (jnp.maximum(i * blocks_per_tile - 1, 0), 0)),
                  _const_spec((CONV_K, CV_WIDTH)), _const_spec((1, CV_WIDTH)),
                  _const_spec((1, CV_WIDTH)), _const_spec((1, CV_WIDTH))],
        out_specs=pl.BlockSpec((TM, CV_WIDTH), lambda i: (i, 0)),
        out_shape=jax.ShapeDtypeStruct((M_ALL, CV_WIDTH), BF16),
        scratch_shapes=[pltpu.VMEM((TM + HALO, CV_WIDTH), F32), pltpu.VMEM((CHUNK, CV_WIDTH), F32)],
        compiler_params=_cparams(1), name="conv_prompt")(c, c, w, b, g, be)


CONV_TB = 16


def _conv_s_body(cache_ref, c_ref, w_ref, b_ref, g_ref, be_ref, state_in, o_ref, state_ref):
    n_hist = CONV_K - 1
    hist = jnp.sum(cache_ref[...] * w_ref[0:n_hist, :][None], axis=1)
    acc = hist + c_ref[...] * w_ref[n_hist:CONV_K, :] + b_ref[...]
    h = _ln(acc, g_ref[...], be_ref[...])
    o_ref[...] = (h * jax.nn.sigmoid(h)).astype(BF16)
    state_ref[:, 0:n_hist - 1, :] = cache_ref[:, 1:n_hist, :]
    for t in range(CONV_TB):
        state_ref[t, n_hist - 1:n_hist, :] = c_ref[t:t + 1, :]


def _conv_sample(cache, l, c, w, b, g, be, states):
    first = M_PROMPT // CONV_TB
    hist = pl.BlockSpec((None, CONV_TB, CONV_K - 1, CV_WIDTH), lambda i: (l, i, 0, 0))
    return pl.pallas_call(
        _conv_s_body, grid=(DEC_BATCH // CONV_TB,),
        in_specs=[hist, pl.BlockSpec((CONV_TB, CV_WIDTH), lambda i: (first + i, 0)),
                  _const_spec((CONV_K, CV_WIDTH)), _const_spec((1, CV_WIDTH)),
                  _const_spec((1, CV_WIDTH)), _const_spec((1, CV_WIDTH)),
                  pl.BlockSpec(memory_space=pl.ANY)],
        out_specs=[pl.BlockSpec((CONV_TB, CV_WIDTH), lambda i: (i, 0)), hist],
        out_shape=[jax.ShapeDtypeStruct((DEC_BATCH, CV_WIDTH), BF16),
                   jax.ShapeDtypeStruct(states.shape, F32)],
        input_output_aliases={6: 1},
        compiler_params=_cparams(1), name="conv_sample")(cache, c, w, b, g, be, states)


def _merge_body(sg_ref, h_ref, wa_ref, wb_ref, ga_ref, gb_ref, o_ref, wa_s, wb_s):
    @pl.when(pl.program_id(1) == 0)
    def _():
        wa_s[...] = wa_ref[...].astype(BF16)
        wb_s[...] = wb_ref[...].astype(BF16)

    a = jnp.dot(sg_ref[...], wa_s[...], preferred_element_type=F32)
    b = jnp.dot(h_ref[...], wb_s[...], preferred_element_type=F32)
    o_ref[...] = (ga_ref[...].astype(F32) * a + gb_ref[...].astype(F32) * b).astype(BF16)


def _merge(sg, h, wa, wb, l, gates):
    tn, tm = 512, TM_WIDE
    nb = D_MODEL // tn
    lhs = pl.BlockSpec((tm, GM_WIDTH), lambda j, i: (i, 0))
    wspec = pl.BlockSpec((None, GM_WIDTH, tn), lambda j, i: (l, 0, j))
    return pl.pallas_call(
        _merge_body, grid=(nb, M_ALL // tm),
        in_specs=[lhs, lhs, wspec, wspec,
                  pl.BlockSpec((tm, tn), lambda j, i: (i, j)),
                  pl.BlockSpec((tm, tn), lambda j, i: (i, j + nb))],
        out_specs=pl.BlockSpec((tm, tn), lambda j, i: (i, j)),
        out_shape=jax.ShapeDtypeStruct((M_ALL, D_MODEL), BF16),
        scratch_shapes=[pltpu.VMEM((GM_WIDTH, tn), BF16), pltpu.VMEM((GM_WIDTH, tn), BF16)],
        compiler_params=_cparams(2), name="merge")(sg, h, wa, wb, gates, gates)


def _proj_ln_body(a_ref, w_ref, r_ref, g_ref, b_ref, of_ref, ob_ref):
    for c in range(TM // CH):
        rows = pl.ds(c * CH, CH)
        y = jnp.dot(a_ref[rows, :], w_ref[...], preferred_element_type=F32)
        z = _ln(ALPHA * r_ref[rows, :] + y, g_ref[...], b_ref[...])
        of_ref[rows, :] = z
        ob_ref[rows, :] = z.astype(BF16)


def _proj_ln(a, w, r, g, b, name):
    row = pl.BlockSpec((TM, D_MODEL), lambda i: (i, 0))
    return pl.pallas_call(
        _proj_ln_body, grid=(N_MT,),
        in_specs=[row, _const_spec((D_MODEL, D_MODEL), single=True), row,
                  _const_spec((1, D_MODEL)), _const_spec((1, D_MODEL))],
        out_specs=[row, row],
        out_shape=[jax.ShapeDtypeStruct((M_ALL, D_MODEL), F32),
                   jax.ShapeDtypeStruct((M_ALL, D_MODEL), BF16)],
        compiler_params=_cparams(1), name=name)(a, w, r, g, b)


def _pack_pair(lo, hi):
    lo_b = pltpu.bitcast(lo.astype(BF16).astype(F32), U32)
    hi_b = pltpu.bitcast(hi.astype(BF16).astype(F32), U32)
    return lax.shift_right_logical(lo_b, jnp.uint32(16)) | (hi_b & jnp.uint32(0xFFFF0000))


def _unpack_pair(p):
    lo = pltpu.bitcast(lax.shift_left(p, jnp.uint32(16)), F32)
    hi = pltpu.bitcast(p & jnp.uint32(0xFFFF0000), F32)
    return lo.astype(BF16), hi.astype(BF16)


def _proj_ln_route_body(a_ref, w_ref, r_ref, g_ref, b_ref, wr_ref, of_ref, xp_ref, lg_ref):
    half = D_MODEL // 2
    for c in range(TM // CH):
        rows = pl.ds(c * CH, CH)
        y = jnp.dot(a_ref[rows, :], w_ref[...], preferred_element_type=F32)
        z = _ln(ALPHA * r_ref[rows, :] + y, g_ref[...], b_ref[...])
        of_ref[rows, :] = z
        packed = _pack_pair(z[:, :half], z[:, half:])
        for s in range(XP_S):
            xp_ref[pl.ds(c * CH * XP_S + s, CH, stride=XP_S), :] = packed[:, s * LANES:(s + 1) * LANES]
    nt = (((1,), (1,)), ((), ()))
    wr = wr_ref[...]
    wr_hi = wr.astype(BF16)
    wr_lo = (wr - wr_hi.astype(F32)).astype(BF16)
    z = of_ref[...]
    z_hi = z.astype(BF16)
    z_lo = (z - z_hi.astype(F32)).astype(BF16)
    both = lax.dot_general(jnp.concatenate([wr_hi, wr_lo], axis=0), z_hi, nt, preferred_element_type=F32)
    tail = lax.dot_general(wr_hi, z_lo, nt, preferred_element_type=F32)
    lg_ref[...] = both[:N_EXPERTS] + both[N_EXPERTS:] + tail


def _proj_ln_route(a, w, r, g, b, wr_t):
    row = pl.BlockSpec((TM, D_MODEL), lambda i: (i, 0))
    return pl.pallas_call(
        _proj_ln_route_body, grid=(N_MT,),
        in_specs=[row, _const_spec((D_MODEL, D_MODEL), single=True), row,
                  _const_spec((1, D_MODEL)), _const_spec((1, D_MODEL)),
                  _const_spec((N_EXPERTS, D_MODEL))],
        out_specs=[row, pl.BlockSpec((TM * XP_S, LANES), lambda i: (i, 0)),
                   pl.BlockSpec((N_EXPERTS, TM), lambda i: (0, i))],
        out_shape=[jax.ShapeDtypeStruct((M_ALL, D_MODEL), F32),
                   jax.ShapeDtypeStruct((M_ALL * XP_S, LANES), U32),
                   jax.ShapeDtypeStruct((N_EXPERTS, M_ALL), F32)],
        compiler_params=_cparams(1), name="xo_ln_route")(a, w, r, g, b, wr_t)


def _kv_body(x_ref, w_ref, of_ref, ob_ref):
    y = jnp.dot(x_ref[...], w_ref[...].astype(BF16), preferred_element_type=F32)
    of_ref[...] = y
    ob_ref[...] = y.astype(BF16)


def _kv(mem_b, w, l):
    tn = 1024
    m = mem_b.shape[0]
    out = pl.BlockSpec((m, tn), lambda j: (0, j))
    return pl.pallas_call(
        _kv_body, grid=(D_MODEL // tn,),
        in_specs=[_const_spec((m, D_MODEL)), pl.BlockSpec((None, D_MODEL, tn), lambda j: (l, 0, j))],
        out_specs=[out, out],
        out_shape=[jax.ShapeDtypeStruct((m, D_MODEL), F32), jax.ShapeDtypeStruct((m, D_MODEL), BF16)],
        compiler_params=_cparams(1), name="mem_kv")(mem_b, w)


TQ = 512


def _attn_body(q_ref, k_ref, v_ref, o_ref):
    scale = X_HEAD_DIM ** -0.5
    n_prompt_tiles = M_PROMPT // TQ

    @pl.when(pl.program_id(0) < n_prompt_tiles)
    def _():
        for h in range(X_HEADS):
            cols = pl.ds(h * X_HEAD_DIM, X_HEAD_DIM)
            s = lax.dot_general(q_ref[:, cols], k_ref[:, cols], (((1,), (1,)), ((), ())),
                                preferred_element_type=F32) * scale
            s = s - jnp.max(s, axis=-1, keepdims=True)
            e = jnp.exp(s)
            p = e / jnp.sum(e, axis=-1, keepdims=True)
            o = jnp.dot(p.astype(BF16), v_ref[:, cols], preferred_element_type=F32)
            o_ref[:, cols] = o.astype(BF16)

    @pl.when(pl.program_id(0) >= n_prompt_tiles)
    def _():
        o_ref[...] = jnp.zeros(o_ref.shape, BF16)


def _attn_prompt(q, k, v):
    per_b = SEQ // TQ
    last = M_PROMPT // TQ - 1
    kv = pl.BlockSpec((MEM_LEN, D_MODEL), lambda i: (jnp.minimum(i, last) // per_b, 0))
    return pl.pallas_call(
        _attn_body, grid=(pl.cdiv(M_ALL, TQ),),
        in_specs=[pl.BlockSpec((TQ, D_MODEL), lambda i: (jnp.minimum(i, last), 0)), kv, kv],
        out_specs=pl.BlockSpec((TQ, D_MODEL), lambda i: (i, 0)),
        out_shape=jax.ShapeDtypeStruct((M_ALL, D_MODEL), BF16),
        compiler_params=_cparams(1), name="attn_prompt")(q, k, v)


ATT_TB = 4


def _attn_s_body(q_ref, k_ref, v_ref, o_ref):
    scale = X_HEAD_DIM ** -0.5
    for t in range(ATT_TB):
        q = q_ref[t].astype(F32)
        s = jnp.sum(k_ref[t] * q[None], axis=-1, keepdims=True) * scale
        s = s - jnp.max(s, axis=0, keepdims=True)
        e = jnp.exp(s)
        p = e / jnp.sum(e, axis=0, keepdims=True)
        o_ref[t] = jnp.sum(p * v_ref[t], axis=0)


def _attn_sample(q_s, cache_k, cache_v, l):
    kv = pl.BlockSpec((None, ATT_TB, MEM_LEN, X_HEADS, X_HEAD_DIM), lambda i: (l, i, 0, 0, 0))
    qo = pl.BlockSpec((ATT_TB, X_HEADS, X_HEAD_DIM), lambda i: (i, 0, 0))
    return pl.pallas_call(
        _attn_s_body, grid=(DEC_BATCH // ATT_TB,),
        in_specs=[qo, kv, kv], out_specs=qo,
        out_shape=jax.ShapeDtypeStruct((DEC_BATCH, X_HEADS, X_HEAD_DIM), F32),
        compiler_params=_cparams(1), name="attn_sample")(q_s, cache_k, cache_v)


def _router_body(lg_ref, br_ref, w_ref, p_ref, cnt_ref, sel_ref, rank_ref):
    aff = jax.nn.sigmoid(lg_ref[...])
    sel = aff + br_ref[...]
    srow = [sel[e:e + 1, :] for e in range(N_EXPERTS)]
    arow = [aff[e:e + 1, :] for e in range(N_EXPERTS)]

    scores = []
    for g in range(N_GROUPS):
        q = srow[g * EPG:(g + 1) * EPG]
        best = None
        for a in range(EPG):
            for b in range(a + 1, EPG):
                pair = q[a] + q[b]
                best = pair if best is None else jnp.maximum(best, pair)
        scores.append(best)
    top = functools.reduce(jnp.maximum, scores)
    taken = jnp.zeros_like(top, dtype=jnp.bool_)
    chosen = []
    for g in range(N_GROUPS):
        is_g = jnp.logical_and(scores[g] == top, jnp.logical_not(taken))
        taken = jnp.logical_or(taken, is_g)
        chosen.append(is_g)

    mask = []
    for g in range(N_GROUPS):
        q = srow[g * EPG:(g + 1) * EPG]
        for a in range(EPG):
            rank = jnp.zeros_like(top)
            for b in range(EPG):
                if b == a:
                    continue
                ahead = q[b] > q[a]
                if b < a:
                    ahead = jnp.logical_or(ahead, q[b] == q[a])
                rank = rank + ahead.astype(F32)
            mask.append(jnp.logical_and(chosen[g], rank < 2.0))

    wsel = [jnp.where(mask[e], arow[e], 0.0) for e in range(N_EXPERTS)]
    den = functools.reduce(lambda x, y: x + y, wsel)
    for e in range(N_EXPERTS):
        sel_ref[e:e + 1, :] = mask[e].astype(F32)

    tri = (lax.broadcasted_iota(I32, (LANES, LANES), 0) <=
           lax.broadcasted_iota(I32, (LANES, LANES), 1)).astype(BF16)
    run = jnp.zeros((N_EXPERTS, 1), F32)
    for blk in range(M_ALL // LANES):
        cols = pl.ds(blk * LANES, LANES)
        m = sel_ref[:, cols]
        incl = jnp.dot(m.astype(BF16), tri, preferred_element_type=F32)
        rank_ref[:, cols] = run + incl - m
        run = run + incl[:, LANES - 1:LANES]
    padded = jnp.floor((run + (TME - 1)) * (1.0 / TME)) * TME
    low = (lax.broadcasted_iota(I32, (N_EXPERTS, N_EXPERTS), 0) >
           lax.broadcasted_iota(I32, (N_EXPERTS, N_EXPERTS), 1)).astype(BF16)
    start = jnp.dot(low, jnp.broadcast_to(padded, (N_EXPERTS, LANES)).astype(BF16),
                    preferred_element_type=F32)
    cnt_ref[...] = jnp.broadcast_to(run, (N_EXPERTS, LANES))

    seen = jnp.zeros_like(taken)
    w0 = jnp.zeros_like(top)
    w1 = jnp.zeros_like(top)
    p0 = jnp.zeros_like(top)
    p1 = jnp.zeros_like(top)
    for e in range(N_EXPERTS):
        we = wsel[e] / den
        dest = rank_ref[e:e + 1, :] + start[e:e + 1, 0:1]
        first = jnp.logical_and(mask[e], jnp.logical_not(seen))
        second = jnp.logical_and(mask[e], seen)
        seen = jnp.logical_or(seen, mask[e])
        w0 = jnp.where(first, we, w0)
        w1 = jnp.where(second, we, w1)
        p0 = jnp.where(first, dest, p0)
        p1 = jnp.where(second, dest, p1)
    w_ref[0:1, :] = w0
    w_ref[1:2, :] = w1
    p_ref[0:1, :] = p0.astype(I32)
    p_ref[1:2, :] = p1.astype(I32)


def _router(logits_t, b_router):
    return pl.pallas_call(
        _router_body, grid=(1,),
        in_specs=[_const_spec((N_EXPERTS, M_ALL)), _const_spec((N_EXPERTS, 1))],
        out_specs=[_const_spec((2, M_ALL)), _const_spec((2, M_ALL)), _const_spec((N_EXPERTS, LANES))],
        out_shape=[jax.ShapeDtypeStruct((2, M_ALL), F32), jax.ShapeDtypeStruct((2, M_ALL), I32),
                   jax.ShapeDtypeStruct((N_EXPERTS, LANES), F32)],
        scratch_shapes=[pltpu.VMEM((N_EXPERTS, M_ALL), F32), pltpu.VMEM((N_EXPERTS, M_ALL), F32)],
        compiler_params=_cparams(1), name="router")(logits_t, b_router)


def _scatter_body(p_ref, cnt_ref, start_ref, x_ref, o_hbm, zero_ref, sem, zsem):
    i = pl.program_id(0)

    def pad_copies(fn):
        for e in range(N_EXPERTS):
            n_tok = cnt_ref[e]
            n_pad = (TME - n_tok % TME) % TME
            cur = start_ref[e] + n_tok
            bit = TME // 2
            while bit >= 1:
                take = (n_pad & bit) != 0

                @pl.when(take)
                def _(cur=cur, bit=bit):
                    fn(pltpu.make_async_copy(zero_ref.at[pl.ds(0, bit)], o_hbm.at[pl.ds(cur, bit)], zsem))

                cur = cur + jnp.where(take, bit, 0)
                bit //= 2

    last = N_EXPERTS - 1
    n_used = (start_ref[last] + cnt_ref[last] + TME - 1) // TME

    def tail_copies(fn):
        def one(ti, carry):
            fn(pltpu.make_async_copy(zero_ref, o_hbm.at[pl.ds(ti * TME, TME)], zsem))
            return carry
        lax.fori_loop(n_used, N_ET, one, 0)

    @pl.when(i == 0)
    def _():
        zero_ref[...] = jnp.zeros(zero_ref.shape, U32)
        pad_copies(lambda cp: cp.start())
        tail_copies(lambda cp: cp.start())

    base = i * TM

    def issue(r, carry):
        t = base + r
        src = x_ref.at[pl.ds(pl.multiple_of(r * XP_S, XP_S), XP_S), :]
        pltpu.make_async_copy(src, o_hbm.at[p_ref[t]], sem).start()
        pltpu.make_async_copy(src, o_hbm.at[p_ref[M_ALL + t]], sem).start()
        return carry

    lax.fori_loop(0, TM, issue, 0)
    done = pltpu.make_async_copy(x_ref, x_ref, sem)
    done.wait()
    done.wait()

    @pl.when(i == N_MT - 1)
    def _():
        pad_copies(lambda cp: cp.wait())
        tail_copies(lambda cp: cp.wait())


def _scatter(p_flat, cnt, start, xp):
    return pl.pallas_call(
        _scatter_body,
        grid_spec=pltpu.PrefetchScalarGridSpec(
            num_scalar_prefetch=3, grid=(N_MT,),
            in_specs=[pl.BlockSpec((TM * XP_S, LANES), lambda i, p, c, s: (i, 0))],
            out_specs=pl.BlockSpec(memory_space=pl.ANY),
            scratch_shapes=[pltpu.VMEM((TME, XP_S, LANES), U32),
                            pltpu.SemaphoreType.DMA(()), pltpu.SemaphoreType.DMA(())]),
        out_shape=jax.ShapeDtypeStruct((M_SORT, XP_S, LANES), U32),
        compiler_params=_cparams(1), name="moe_scatter")(p_flat, cnt, start, xp)


W_CHUNK_ROWS = 512
W_CHUNK_COLS = 1024
N_WCHUNK = 3 * (D_MODEL * D_EXPERT) // (W_CHUNK_ROWS * W_CHUNK_COLS)
W_RING = 4
CAST_ROWS = 128


def _experts_body(l, eff_ref, exp_ref, valid_ref, first_ref, slot_ref, nxt_ref, hasn_ref, lo_ref, hi_ref,
                  xs_ref, wg_hbm, wu_hbm, wd_hbm, o_ref, wg_b, wu_b, wd_b, stage, sems):
    i = pl.program_id(0)
    per_mat = N_WCHUNK // 3
    d_cols = D_MODEL // W_CHUNK_COLS

    def chunk_copy(e, c):
        if c < per_mat:
            src = wg_hbm.at[l, e, pl.ds(c * W_CHUNK_ROWS, W_CHUNK_ROWS), :]
        elif c < 2 * per_mat:
            src = wu_hbm.at[l, e, pl.ds((c - per_mat) * W_CHUNK_ROWS, W_CHUNK_ROWS), :]
        else:
            r, q = divmod(c - 2 * per_mat, d_cols)
            src = wd_hbm.at[l, e, pl.ds(r * W_CHUNK_ROWS, W_CHUNK_ROWS), pl.ds(q * W_CHUNK_COLS, W_CHUNK_COLS)]
        return pltpu.make_async_copy(src, stage.at[c % W_RING], sems.at[c % W_RING])

    def cast_chunk(s, c):
        if c < per_mat:
            dst, r0, c0 = wg_b, c * W_CHUNK_ROWS, 0
        elif c < 2 * per_mat:
            dst, r0, c0 = wu_b, (c - per_mat) * W_CHUNK_ROWS, 0
        else:
            r, q = divmod(c - 2 * per_mat, d_cols)
            dst, r0, c0 = wd_b, r * W_CHUNK_ROWS, q * W_CHUNK_COLS

        def piece(k, carry):
            rows = pl.multiple_of(k * CAST_ROWS, CAST_ROWS)
            dst[s, pl.ds(r0 + rows, CAST_ROWS), pl.ds(c0, W_CHUNK_COLS)] = (
                stage[c % W_RING, pl.ds(rows, CAST_ROWS), :].astype(BF16))
            return carry

        lax.fori_loop(0, W_CHUNK_ROWS // CAST_ROWS, piece, 0)

    def finish_chunk(e, s, c):
        chunk_copy(e, c).wait()
        cast_chunk(s, c)
        if c + W_RING < N_WCHUNK:
            chunk_copy(e, c + W_RING).start()

    valid = valid_ref[i] == 1
    slot = slot_ref[i]

    @pl.when(i == 0)
    def _():
        e0 = exp_ref[0]
        for c in range(W_RING):
            chunk_copy(e0, c).start()
        for c in range(N_WCHUNK):
            finish_chunk(e0, slot, c)

    stream = jnp.logical_and(valid, hasn_ref[i] == 1)
    e_next = nxt_ref[i]

    @pl.when(jnp.logical_and(stream, first_ref[i] == 1))
    def _():
        for c in range(W_RING):
            chunk_copy(e_next, c).start()

    @pl.when(valid)
    def _():
        packed = jnp.concatenate([xs_ref[pl.ds(s, TME, stride=XP_S), :] for s in range(XP_S)], axis=1)
        lo, hi = _unpack_pair(packed)
        x = jnp.concatenate([lo, hi], axis=1)
        g = jnp.dot(x, wg_b[slot], preferred_element_type=F32)
        u = jnp.dot(x, wu_b[slot], preferred_element_type=F32)
        h = (g * jax.nn.sigmoid(g) * u).astype(BF16)
        y = jnp.dot(h, wd_b[slot], preferred_element_type=F32)
        half = D_MODEL // 2
        packed_y = _pack_pair(y[:, :half], y[:, half:])
        for s in range(XP_S):
            o_ref[pl.ds(s, TME, stride=XP_S), :] = packed_y[:, s * LANES:(s + 1) * LANES]

    @pl.when(jnp.logical_not(valid))
    def _():
        o_ref[...] = jnp.zeros(o_ref.shape, U32)

    for c in range(N_WCHUNK):
        @pl.when(jnp.logical_and(stream, jnp.logical_and(lo_ref[i] <= c, c < hi_ref[i])))
        def _(c=c):
            finish_chunk(e_next, 1 - slot, c)


def _experts(maps, xs2, w_gate, w_up, w_down, l):
    n_pre = len(maps)
    tile = lambda i, *m: (m[0][i], 0)
    return pl.pallas_call(
        functools.partial(_experts_body, l),
        grid_spec=pltpu.PrefetchScalarGridSpec(
            num_scalar_prefetch=n_pre, grid=(N_ET,),
            in_specs=[pl.BlockSpec((TME * XP_S, LANES), tile),
                      pl.BlockSpec(memory_space=pl.ANY), pl.BlockSpec(memory_space=pl.ANY),
                      pl.BlockSpec(memory_space=pl.ANY)],
            out_specs=pl.BlockSpec((TME * XP_S, LANES), lambda i, *m: (i, 0)),
            scratch_shapes=[pltpu.VMEM((2, D_MODEL, D_EXPERT), BF16), pltpu.VMEM((2, D_MODEL, D_EXPERT), BF16),
                            pltpu.VMEM((2, D_EXPERT, D_MODEL), BF16),
                            pltpu.VMEM((W_RING, W_CHUNK_ROWS, W_CHUNK_COLS), F32),
                            pltpu.SemaphoreType.DMA((W_RING,))]),
        out_shape=jax.ShapeDtypeStruct((M_SORT * XP_S, LANES), U32),
        compiler_params=_cparams(1), name="experts")(*maps, xs2, w_gate, w_up, w_down)


def _unpack_rows(buf, slot, row0, n):
    p = jnp.concatenate([buf[slot, pl.ds(row0 * XP_S + s, n, stride=XP_S), :] for s in range(XP_S)], axis=1)
    lo = pltpu.bitcast(lax.shift_left(p, jnp.uint32(16)), F32)
    hi = pltpu.bitcast(p & jnp.uint32(0xFFFF0000), F32)
    return jnp.concatenate([lo, hi], axis=1)


def _combine_body(p_ref, w_ref, x_ref, y_hbm, g_ref, b_ref, of_ref, ob_ref, s0, s1, sem0, sem1):
    i = pl.program_id(0)
    cur = i % 2

    def gather_tile(tile, buf):
        base = tile * TM

        def issue(r, carry):
            t = base + r
            dst = pl.ds(pl.multiple_of(r * XP_S, XP_S), XP_S)
            pltpu.make_async_copy(y_hbm.at[p_ref[t]], s0.at[buf, dst], sem0.at[buf]).start()
            pltpu.make_async_copy(y_hbm.at[p_ref[M_ALL + t]], s1.at[buf, dst], sem1.at[buf]).start()
            return carry

        lax.fori_loop(0, TM, issue, 0)

    @pl.when(i == 0)
    def _():
        gather_tile(0, 0)

    @pl.when(i + 1 < N_MT)
    def _():
        gather_tile(i + 1, 1 - cur)

    pltpu.make_async_copy(s0.at[cur], s0.at[cur], sem0.at[cur]).wait()
    pltpu.make_async_copy(s1.at[cur], s1.at[cur], sem1.at[cur]).wait()

    for c in range(TM // CHUNK):
        rows = pl.ds(c * CHUNK, CHUNK)
        y0 = _unpack_rows(s0, cur, c * CHUNK, CHUNK)
        y1 = _unpack_rows(s1, cur, c * CHUNK, CHUNK)
        w = w_ref[rows, :]
        z = ALPHA * x_ref[rows, :] + w[:, 0:1] * y0 + w[:, 1:2] * y1
        z = _ln(z, g_ref[...], b_ref[...])
        of_ref[rows, :] = z
        ob_ref[rows, :] = z.astype(BF16)


def _combine(p_flat, w_tok, x2f, ys3, g, b):
    row = pl.BlockSpec((TM, D_MODEL), lambda i, p: (i, 0))
    return pl.pallas_call(
        _combine_body,
        grid_spec=pltpu.PrefetchScalarGridSpec(
            num_scalar_prefetch=1, grid=(N_MT,),
            in_specs=[pl.BlockSpec((TM, 2), lambda i, p: (i, 0)), row,
                      pl.BlockSpec(memory_space=pl.ANY),
                      pl.BlockSpec((1, D_MODEL), lambda i, p: (0, 0)),
                      pl.BlockSpec((1, D_MODEL), lambda i, p: (0, 0))],
            out_specs=[row, row],
            scratch_shapes=[pltpu.VMEM((2, TM * XP_S, LANES), U32), pltpu.VMEM((2, TM * XP_S, LANES), U32),
                            pltpu.SemaphoreType.DMA((2,)), pltpu.SemaphoreType.DMA((2,))]),
        out_shape=[jax.ShapeDtypeStruct((M_ALL, D_MODEL), F32),
                   jax.ShapeDtypeStruct((M_ALL, D_MODEL), BF16)],
        compiler_params=_cparams(1), name="moe_combine")(p_flat, w_tok, x2f, ys3, g, b)


def _row(v):
    return v.reshape(1, -1)


def _tile_maps(cnt):
    tiles = (cnt + TME - 1) // TME
    ends = jnp.cumsum(tiles)
    total = ends[-1]
    ids = jnp.arange(N_ET, dtype=I32)
    eff = jnp.minimum(ids, total - 1)
    expert = jnp.minimum(jnp.sum((eff[:, None] >= ends[None, :]).astype(I32), axis=1), N_EXPERTS - 1)
    valid = ids < total
    prev = jnp.concatenate([jnp.full((1,), -1, I32), expert[:-1]])
    first = jnp.logical_and(valid, expert != prev)
    start = ((ends - tiles) * TME).astype(I32)
    eids = jnp.arange(N_EXPERTS, dtype=I32)
    used = tiles > 0
    slot_e = (jnp.cumsum(used.astype(I32)) - 1) % 2
    later = jnp.logical_and(eids[None, :] > eids[:, None], used[None, :])
    nxt_e = jnp.min(jnp.where(later, eids[None, :], N_EXPERTS), axis=1)
    has_next = nxt_e[expert] < N_EXPERTS
    n_own = jnp.maximum(tiles[expert], 1)
    j_own = eff - (ends[expert] - tiles[expert])
    lo = jnp.where(valid, (N_WCHUNK * j_own) // n_own, 0)
    hi = jnp.where(valid, (N_WCHUNK * (j_own + 1)) // n_own, 0)
    maps = (eff, expert, valid, first, slot_e[expert], jnp.minimum(nxt_e[expert], N_EXPERTS - 1),
            jnp.logical_and(valid, has_next), lo, hi)
    return tuple(m.astype(I32) for m in maps), start


def kernel(x_prompt, x_sample, mem_prompt, cache_conv, cache_mem_k, cache_mem_v, ln_in_g, ln_in_b, w_in, b_in, gm_ln_g, gm_ln_b, gm_w_s, gm_b_s, w_a, conv_w, conv_b, cv_ln_g, cv_ln_b, w_b, w_o, ln1_g, ln1_b, w_q, w_k, w_v, w_xo, ln2_g, ln2_b, w_router, b_router, w_gate, w_up, w_down, ln3_g, ln3_b):
    xf, xb = _ln_in(x_prompt.reshape(M_PROMPT, D_MODEL), x_sample.reshape(DEC_BATCH, D_MODEL),
                    _row(ln_in_g), _row(ln_in_b))
    conv_states = jnp.zeros(cache_conv.shape, F32)
    mem_b = mem_prompt.reshape(BATCH * MEM_LEN, D_MODEL).astype(BF16)
    wr_t = w_router.T
    br = b_router.reshape(N_EXPERTS, 1)

    gelu = jax.nn.gelu
    o_u, o_v, o_c, o_g = 0, GM_WIDTH, 2 * GM_WIDTH, 2 * GM_WIDTH + 2 * CV_WIDTH
    zero_b = jnp.zeros((1, D_MODEL), F32)
    mk_l, mv_l, cp_l, vs_l = [], [], [], []
    for l in range(DEPTH):
        bi = b_in[l]
        u = _proj(xb, w_in, l, o_u, GM_WIDTH, _row(bi[o_u:o_v]),
                  lambda a: (gelu(a),), [BF16], tn=512, tm=TM_WIDE, name="proj_u")[0]
        v = _proj(xb, w_in, l, o_v, GM_WIDTH, _row(bi[o_v:o_c]),
                  lambda a, g, b: (_ln(gelu(a), g, b),), [F32], tn=GM_WIDTH, tm=TM_MID,
                  params=(_row(gm_ln_g[l]), _row(gm_ln_b[l])), name="proj_v")[0]
        c = _glu(xb, w_in, l, o_c, _row(bi[o_c:o_c + CV_WIDTH]), _row(bi[o_c + CV_WIDTH:o_g]))
        gates = _proj(xb, w_in, l, o_g, 2 * D_MODEL, _row(bi[o_g:]),
                      lambda a: (jax.nn.sigmoid(a),), [BF16], tn=512, tm=TM_WIDE, name="proj_gates")[0]
        sg = _gate(u, v, gm_w_s[l], gm_b_s[l].T)
        cw, cb = conv_w[l], _row(conv_b[l])
        cg, cbe = _row(cv_ln_g[l]), _row(cv_ln_b[l])
        h = _conv(c, cw, cb, cg, cbe)
        h_s, conv_states = _conv_sample(cache_conv, l, c, cw, cb, cg, cbe, conv_states)
        h = lax.dynamic_update_slice(h, h_s, (M_PROMPT, 0))
        merged = _merge(sg, h, w_a, w_b, l, gates)
        x1f, x1b = _proj_ln(merged, w_o[l].astype(BF16), xf, _row(ln1_g[l]), _row(ln1_b[l]), "wo_ln")

        q = _proj(x1b, w_q, l, 0, D_MODEL, zero_b, lambda a: (a,), [BF16], tn=512, tm=TM_WIDE, name="proj_q")[0]
        mk_f, mk_b = _kv(mem_b, w_k, l)
        mv_f, mv_b = _kv(mem_b, w_v, l)
        o_all = _attn_prompt(q, mk_b, mv_b)
        q_s = q[M_PROMPT:].astype(F32).reshape(DEC_BATCH, X_HEADS, X_HEAD_DIM)
        o_s = _attn_sample(q_s, cache_mem_k, cache_mem_v, l)
        o_all = lax.dynamic_update_slice(o_all, o_s.reshape(DEC_BATCH, D_MODEL).astype(BF16), (M_PROMPT, 0))
        x2f, xp, logits_t = _proj_ln_route(o_all, w_xo[l].astype(BF16), x1f, _row(ln2_g[l]), _row(ln2_b[l]), wr_t)

        w_tok, p_tok, cnt_f = _router(logits_t, br)
        cnt = cnt_f[:, 0].astype(I32)
        maps, start = _tile_maps(cnt)
        p_flat = p_tok.reshape(2 * M_ALL)
        xs = _scatter(p_flat, cnt, start, xp)
        ys = _experts(maps, xs.reshape(M_SORT * XP_S, LANES), w_gate, w_up, w_down, l)
        xf, xb = _combine(p_flat, w_tok.T, x2f, ys.reshape(M_SORT, XP_S, LANES), _row(ln3_g[l]), _row(ln3_b[l]))

        mk_l.append(mk_f.reshape(BATCH, MEM_LEN, X_HEADS, X_HEAD_DIM))
        mv_l.append(mv_f.reshape(BATCH, MEM_LEN, X_HEADS, X_HEAD_DIM))
        cp_l.append(c[:M_PROMPT].reshape(BATCH, SEQ, CV_WIDTH)[:, SEQ - (CONV_K - 1):])
        vs_l.append(v[M_PROMPT:].reshape(DEC_BATCH, 1, GM_WIDTH))

    return (xf[:M_PROMPT].reshape(BATCH, SEQ, D_MODEL), xf[M_PROMPT:].reshape(DEC_BATCH, 1, D_MODEL),
            jnp.stack(mk_l), jnp.stack(mv_l), jnp.stack(cp_l), conv_states, jnp.stack(vs_l))
```
